```python
import jax, jax.numpy as jnp
from jax import lax
import numpy as np

D_MODEL = 2048
BATCH = 2
SEQ = 4096
DEPTH = 2

HEAD_DIM = 128
ROPE_THETA = 10000.0
EPS = 1e-6
MASK_VALUE = -1e30
FORCE_VALUE = 1e9
MIN_FORGET = 1e-6
NSA_HEADS = 6
NSA_KV_HEADS = 2
NSA_REP = NSA_HEADS // NSA_KV_HEADS
CMP_BLOCK = 32
CMP_STRIDE = 16
SEL_BLOCK = 64
N_SEL = 16
WINDOW = 512
Q_BLOCK = 128
GDN_HEADS = 5
CONV_WIDTH = 4
GDN_CHUNK = 64
HGRN_HEADS = 5
HGRN_CHUNK = 64
D_FF = 5632

NSA_WIDTH = NSA_HEADS * HEAD_DIM
KV_WIDTH = NSA_KV_HEADS * HEAD_DIM
GDN_WIDTH = GDN_HEADS * HEAD_DIM
HGRN_WIDTH = HGRN_HEADS * HEAD_DIM
MIX_WIDTH = NSA_WIDTH + GDN_WIDTH + HGRN_WIDTH
IN_SIZES = (NSA_WIDTH, KV_WIDTH, KV_WIDTH, KV_WIDTH, KV_WIDTH, KV_WIDTH, KV_WIDTH, 3 * NSA_HEADS,
            GDN_WIDTH, GDN_WIDTH, GDN_WIDTH, GDN_HEADS, GDN_HEADS, GDN_WIDTH,
            HGRN_WIDTH, HGRN_WIDTH, HGRN_WIDTH, HGRN_WIDTH)
IN_WIDTH = NSA_WIDTH + 6 * KV_WIDTH + 3 * NSA_HEADS + 4 * GDN_WIDTH + 2 * GDN_HEADS + 4 * HGRN_WIDTH

kernel_name = 'hymba_nsa_gdn_hgrn2_macaron'


def rms_norm(x, g):
    xf = x.astype(jnp.float32)
    y = xf * lax.rsqrt(jnp.mean(xf * xf, axis=-1, keepdims=True) + EPS)
    return (y * g.astype(jnp.float32)).astype(x.dtype)


def l2norm(x):
    xf = x.astype(jnp.float32)
    return xf * lax.rsqrt(jnp.sum(xf * xf, axis=-1, keepdims=True) + EPS)


def swiglu(x, w_gate, w_up, w_down):
    return (jax.nn.silu(x @ w_gate) * (x @ w_up)) @ w_down


def split_cols(p, sizes):
    offs = np.cumsum(np.array(sizes))[:-1].tolist()
    return jnp.split(p, offs, axis=-1)


def rope_tables(seq):
    inv = 1.0 / (ROPE_THETA ** (jnp.arange(0, HEAD_DIM, 2, dtype=jnp.float32) / HEAD_DIM))
    ang = jnp.arange(seq, dtype=jnp.float32)[:, None] * inv[None, :]
    return jnp.cos(ang), jnp.sin(ang)


def rope(x, cos, sin):
    xf = x.astype(jnp.float32)
    x1, x2 = jnp.split(xf, 2, axis=-1)
    c = cos[None, :, None, :]
    s = sin[None, :, None, :]
    return jnp.concatenate([x1 * c - x2 * s, x2 * c + x1 * s], axis=-1).astype(x.dtype)


def masked_softmax(s, mask):
    s = jnp.where(mask, s.astype(jnp.float32), MASK_VALUE)
    m = jnp.max(s, axis=-1, keepdims=True)
    e = jnp.where(mask, jnp.exp(s - m), 0.0)
    return e / jnp.maximum(jnp.sum(e, axis=-1, keepdims=True), 1e-30)


def masked_decay(diff, causal):
    return jnp.where(causal, jnp.exp(jnp.where(causal, diff, 0.0)), 0.0)


def causal_conv(x, w):
    c = w.shape[1]
    return lax.conv_general_dilated(x, w[:, None, :].astype(x.dtype), window_strides=(1,),
                                    padding=[(CONV_WIDTH - 1, 0)],
                                    dimension_numbers=('NWC', 'WIO', 'NWC'),
                                    feature_group_count=c)


def nsa_mixer(q, k_c, v_c, k_s, v_s, k_w, v_w, gate_logits, pe_k, pe_v, w_k1, w_k2, w_v1, w_v2, cos, sin):
    B, S = q.shape[0], q.shape[1]
    G, R, D = NSA_KV_HEADS, NSA_REP, HEAD_DIM
    scale = HEAD_DIM ** -0.5
    q = (rope(q.reshape(B, S, NSA_HEADS, D), cos, sin) * scale).reshape(B, S, G, R, D)
    k_c = rope(k_c.reshape(B, S, G, D), cos, sin)
    k_s = rope(k_s.reshape(B, S, G, D), cos, sin)
    k_w = rope(k_w.reshape(B, S, G, D), cos, sin)
    v_c = v_c.reshape(B, S, G, D)
    v_s = v_s.reshape(B, S, G, D)
    v_w = v_w.reshape(B, S, G, D)

    n_cmp = (S - CMP_BLOCK) // CMP_STRIDE + 1
    tok = jnp.arange(n_cmp)[:, None] * CMP_STRIDE + jnp.arange(CMP_BLOCK)[None, :]

    def compress(t, pe, w1, w2):
        blocks = t[:, tok] + pe[None, None, :, None, :]
        flat = blocks.transpose(0, 3, 1, 2, 4).reshape(B, G, n_cmp, CMP_BLOCK * D)
        return jax.nn.gelu(flat @ w1) @ w2

    kc = compress(k_c, pe_k, w_k1, w_k2)
    vc = compress(v_c, pe_v, w_v1, w_v2)
    cmp_end = jnp.arange(n_cmp) * CMP_STRIDE + CMP_BLOCK - 1

    n_blk = S // SEL_BLOCK
    n_sel = min(N_SEL, n_blk)
    c_start = jnp.arange(n_cmp) * CMP_STRIDE
    s_start = jnp.arange(n_blk) * SEL_BLOCK
    ov = (jnp.minimum(c_start[:, None] + CMP_BLOCK, s_start[None, :] + SEL_BLOCK)
          - jnp.maximum(c_start[:, None], s_start[None, :]))
    overlap = jnp.clip(ov, 0, None).astype(jnp.float32) / CMP_BLOCK
    ksb = k_s.reshape(B, n_blk, SEL_BLOCK, G, D).transpose(0, 3, 1, 2, 4)
    vsb = v_s.reshape(B, n_blk, SEL_BLOCK, G, D).transpose(0, 3, 1, 2, 4)

    kw = jnp.pad(k_w.transpose(0, 2, 1, 3), ((0, 0), (0, 0), (WINDOW, 0), (0, 0)))
    vw = jnp.pad(v_w.transpose(0, 2, 1, 3), ((0, 0), (0, 0), (WINDOW, 0), (0, 0)))

    n_qb = S // Q_BLOCK
    qb = q.reshape(B, n_qb, Q_BLOCK, G, R, D).transpose(1, 0, 3, 4, 2, 5)
    gb = jax.nn.sigmoid(gate_logits.astype(jnp.float32)).reshape(B, n_qb, Q_BLOCK, G, R, 3)
    gb = gb.transpose(1, 0, 3, 4, 2, 5)
    b_idx = jnp.arange(B)[:, None, None, None]
    g_idx = jnp.arange(G)[None, :, None, None]
    blk = jnp.arange(n_blk)
    sel_off = jnp.arange(SEL_BLOCK)
    win_off = jnp.arange(Q_BLOCK + WINDOW)

    def block(args):
        i, qi, gt = args
        t = i * Q_BLOCK + jnp.arange(Q_BLOCK)
        s_c = jnp.einsum('bgrqd,bgnd->bgrqn', qi, kc).astype(jnp.float32)
        p_c = masked_softmax(s_c, cmp_end[None, :] <= t[:, None])
        o_c = jnp.einsum('bgrqn,bgnd->bgrqd', p_c.astype(vc.dtype), vc)
        imp = jnp.einsum('bgqn,nm->bgqm', jnp.sum(p_c, axis=2), overlap)
        cur = t // SEL_BLOCK
        forced = ((blk[None, :] == 0) | (blk[None, :] == cur[:, None])
                  | (blk[None, :] == cur[:, None] - 1))
        future = blk[None, :] * SEL_BLOCK > t[:, None]
        imp = jnp.where(forced, FORCE_VALUE, jnp.where(future, -FORCE_VALUE, imp))
        _, idx = lax.top_k(imp, n_sel)
        k_sel = ksb[b_idx, g_idx, idx].reshape(B, G, Q_BLOCK, n_sel * SEL_BLOCK, D)
        v_sel = vsb[b_idx, g_idx, idx].reshape(B, G, Q_BLOCK, n_sel * SEL_BLOCK, D)
        kpos = (idx[..., None] * SEL_BLOCK + sel_off).reshape(B, G, 1, Q_BLOCK, n_sel * SEL_BLOCK)
        s_s = jnp.einsum('bgrqd,bgqmd->bgrqm', qi, k_sel).astype(jnp.float32)
        p_s = masked_softmax(s_s, kpos <= t[None, None, None, :, None])
        o_s = jnp.einsum('bgrqm,bgqmd->bgrqd', p_s.astype(v_sel.dtype), v_sel)
        k_win = lax.dynamic_slice_in_dim(kw, i * Q_BLOCK, Q_BLOCK + WINDOW, axis=2)
        v_win = lax.dynamic_slice_in_dim(vw, i * Q_BLOCK, Q_BLOCK + WINDOW, axis=2)
        wpos = i * Q_BLOCK - WINDOW + win_off
        mask_w = ((wpos[None, :] <= t[:, None]) & (wpos[None, :] > t[:, None] - WINDOW)
                  & (wpos[None, :] >= 0))
        s_w = jnp.einsum('bgrqd,bgkd->bgrqk', qi, k_win).astype(jnp.float32)
        p_w = masked_softmax(s_w, mask_w)
        o_w = jnp.einsum('bgrqk,bgkd->bgrqd', p_w.astype(v_win.dtype), v_win)
        o = gt[..., 0:1] * o_c + gt[..., 1:2] * o_s + gt[..., 2:3] * o_w
        return o.astype(qi.dtype)

    o = lax.map(block, (jnp.arange(n_qb), qb, gb))
    return o.transpose(1, 0, 4, 2, 3, 5).reshape(B, S, NSA_WIDTH)


def chunk_gated_delta(q, k, v, g, beta):
    B, S, H, D = q.shape
    C = GDN_CHUNK
    n = S // C

    def chunks(a):
        return a.reshape(B, n, C, H, D).transpose(0, 3, 1, 2, 4)

    q, k, v = chunks(q), chunks(k), chunks(v)
    g = g.reshape(B, n, C, H).transpose(0, 3, 1, 2)
    beta = beta.reshape(B, n, C, H).transpose(0, 3, 1, 2)
    gc = jnp.cumsum(g, axis=-1)
    causal = jnp.tril(jnp.ones((C, C), dtype=bool))
    strict = jnp.tril(jnp.ones((C, C), dtype=bool), -1)
    decay = masked_decay(gc[..., :, None] - gc[..., None, :], causal)
    kb = k * beta[..., None]
    A = jnp.where(strict, jnp.einsum('bhnid,bhnjd->bhnij', kb, k) * decay, 0.0)
    rhs = jnp.concatenate([v * beta[..., None], kb * jnp.exp(gc)[..., None]], axis=-1)
    sol = lax.linalg.triangular_solve(A, rhs, left_side=True, lower=True, unit_diagonal=True)
    u, w = sol[..., :D], sol[..., D:]
    attn = jnp.einsum('bhnid,bhnjd->bhnij', q, k) * decay
    q_dec = q * jnp.exp(gc)[..., None]
    k_dec = k * jnp.exp(gc[..., -1:] - gc)[..., None]
    last = jnp.exp(gc[..., -1])

    def step(state, xs):
        u_, w_, qd, kd, at, ld = xs
        v_new = u_ - jnp.einsum('bhcd,bhde->bhce', w_, state)
        o = jnp.einsum('bhcd,bhde->bhce', qd, state) + jnp.einsum('bhij,bhje->bhie', at, v_new)
        state = state * ld[..., None, None] + jnp.einsum('bhcd,bhce->bhde', kd, v_new)
        return state, o

    xs = tuple(jnp.moveaxis(a, 2, 0) for a in (u, w, q_dec, k_dec, attn, last))
    _, o = lax.scan(step, jnp.zeros((B, H, D, D), jnp.float32), xs)
    return o.transpose(1, 0, 3, 2, 4).reshape(B, S, H, D)


def gated_deltanet(q, k, v, a, b, z, w_conv, a_log, dt_bias, norm_w):
    B, S = q.shape[0], q.shape[1]
    H, D = GDN_HEADS, HEAD_DIM
    qkv = jax.nn.silu(causal_conv(jnp.concatenate([q, k, v], axis=-1), w_conv))
    q, k, v = jnp.split(qkv, 3, axis=-1)
    q = l2norm(q.reshape(B, S, H, D)) * (HEAD_DIM ** -0.5)
    k = l2norm(k.reshape(B, S, H, D))
    v = v.reshape(B, S, H, D).astype(jnp.float32)
    beta = jax.nn.sigmoid(b.astype(jnp.float32))
    g = -jnp.exp(a_log.astype(jnp.float32)) * jax.nn.softplus(a.astype(jnp.float32) + dt_bias.astype(jnp.float32))
    o = chunk_gated_delta(q, k, v, g, beta)
    o = rms_norm(o, norm_w) * jax.nn.silu(z.reshape(B, S, H, D).astype(jnp.float32))
    return o.reshape(B, S, GDN_WIDTH).astype(z.dtype)


def chunk_hgrn2(q, k, v, log_f):
    B, S, H, D = q.shape
    C = HGRN_CHUNK
    n = S // C

    def chunks(a):
        return a.reshape(B, n, C, H, D).transpose(1, 0, 3, 2, 4)

    causal = jnp.tril(jnp.ones((C, C), dtype=bool))[:, :, None]

    def step(state, xs):
        q_, k_, v_, lf = xs
        bcum = jnp.cumsum(lf, axis=-2)
        dec = masked_decay(bcum[..., :, None, :] - bcum[..., None, :, :], causal)
        A = jnp.einsum('bhid,bhjd,bhijd->bhij', q_, k_, dec)
        o = jnp.einsum('bhid,bhde->bhie', q_ * jnp.exp(bcum), state) + jnp.einsum('bhij,bhje->bhie', A, v_)
        b_last = bcum[..., -1:, :]
        state = (state * jnp.exp(bcum[..., -1, :])[..., None]
                 + jnp.einsum('bhjd,bhje->bhde', k_ * jnp.exp(b_last - bcum), v_))
        return state, o

    xs = (chunks(q), chunks(k), chunks(v), chunks(log_f))
    _, o = lax.scan(step, jnp.zeros((B, H, D, D), jnp.float32), xs)
    return o.transpose(1, 0, 3, 2, 4).reshape(B, S, H, D)


def hgrn2(q, f, i, g, lb, norm_w):
    B, S = q.shape[0], q.shape[1]
    H, D = HGRN_HEADS, HEAD_DIM
    f_gate = lb + (1.0 - lb) * jax.nn.sigmoid(f.astype(jnp.float32))
    log_f = jnp.log(jnp.maximum(f_gate, MIN_FORGET))
    k = 1.0 - f_gate
    shp = (B, S, H, D)
    o = chunk_hgrn2(q.astype(jnp.float32).reshape(shp), k.reshape(shp),
                    i.astype(jnp.float32).reshape(shp), log_f.reshape(shp))
    o = rms_norm(o, norm_w) * jax.nn.silu(g.reshape(shp).astype(jnp.float32))
    return o.reshape(B, S, HGRN_WIDTH).astype(g.dtype)


def setup_inputs(seed: int = 0) -> dict:
    key = jax.random.key(seed)
    keys = iter(jax.random.split(key, 48))

    def nrm(shape, scale):
        return jax.random.normal(next(keys), shape, jnp.float32) * scale

    def gain(shape):
        return 1.0 + 0.02 * jax.random.normal(next(keys), shape, jnp.float32)

    dt = jax.random.uniform(next(keys), (DEPTH, GDN_HEADS), jnp.float32, 0.001, 0.1)
    return {
        'x': jax.random.normal(next(keys), (BATCH, SEQ, D_MODEL), jnp.float32),
        'ffn1_norm': gain((DEPTH, D_MODEL)),
        'ffn1_gate': nrm((DEPTH, D_MODEL, D_FF), D_MODEL ** -0.5),
        'ffn1_up': nrm((DEPTH, D_MODEL, D_FF), D_MODEL ** -0.5),
        'ffn1_down': nrm((DEPTH, D_FF, D_MODEL), D_FF ** -0.5),
        'mix_norm': gain((DEPTH, D_MODEL)),
        'w_in': nrm((DEPTH, D_MODEL, IN_WIDTH), D_MODEL ** -0.5),
        'w_out': nrm((DEPTH, MIX_WIDTH, D_MODEL), MIX_WIDTH ** -0.5),
        'nsa_pe_k': nrm((DEPTH, CMP_BLOCK, HEAD_DIM), 0.1),
        'nsa_pe_v': nrm((DEPTH, CMP_BLOCK, HEAD_DIM), 0.1),
        'nsa_ck1': nrm((DEPTH, CMP_BLOCK * HEAD_DIM, HEAD_DIM), (CMP_BLOCK * HEAD_DIM) ** -0.5),
        'nsa_ck2': nrm((DEPTH, HEAD_DIM, HEAD_DIM), HEAD_DIM ** -0.5),
        'nsa_cv1': nrm((DEPTH, CMP_BLOCK * HEAD_DIM, HEAD_DIM), (CMP_BLOCK * HEAD_DIM) ** -0.5),
        'nsa_cv2': nrm((DEPTH, HEAD_DIM, HEAD_DIM), HEAD_DIM ** -0.5),
        'gdn_conv': nrm((DEPTH, CONV_WIDTH, 3 * GDN_WIDTH), CONV_WIDTH ** -0.5),
        'gdn_a_log': jnp.log(jax.random.uniform(next(keys), (DEPTH, GDN_HEADS), jnp.float32, 1.0, 16.0)),
        'gdn_dt_bias': dt + jnp.log(-jnp.expm1(-dt)),
        'gdn_norm': gain((DEPTH, HEAD_DIM)),
        'hgrn_lb': nrm((DEPTH, HGRN_WIDTH), 0.5),
        'hgrn_norm': gain((DEPTH, HEAD_DIM)),
        'ffn2_norm': gain((DEPTH, D_MODEL)),
        'ffn2_gate': nrm((DEPTH, D_MODEL, D_FF), D_MODEL ** -0.5),
        'ffn2_up': nrm((DEPTH, D_MODEL, D_FF), D_MODEL ** -0.5),
        'ffn2_down': nrm((DEPTH, D_FF, D_MODEL), D_FF ** -0.5),
        'final_norm': gain((D_MODEL,)),
    }


def reference(x, ffn1_norm, ffn1_gate, ffn1_up, ffn1_down, mix_norm, w_in, w_out,
              nsa_pe_k, nsa_pe_v, nsa_ck1, nsa_ck2, nsa_cv1, nsa_cv2,
              gdn_conv, gdn_a_log, gdn_dt_bias, gdn_norm, hgrn_lb, hgrn_norm,
              ffn2_norm, ffn2_gate, ffn2_up, ffn2_down, final_norm):
    cos, sin = rope_tables(x.shape[1])
    p_lb = jax.nn.softmax(hgrn_lb.astype(jnp.float32), axis=0)
    lb_all = jnp.cumsum(p_lb, axis=0) - p_lb[0:1]
    for l in range(DEPTH):
        x = x + 0.5 * swiglu(rms_norm(x, ffn1_norm[l]), ffn1_gate[l], ffn1_up[l], ffn1_down[l])
        h = rms_norm(x, mix_norm[l])
        (nq, nkc, nvc, nks, nvs, nkw, nvw, ngt,
         gq, gk, gv, ga, gbeta, gz,
         hq, hf, hi, hg) = split_cols(h @ w_in[l], IN_SIZES)
        y_nsa = nsa_mixer(nq, nkc, nvc, nks, nvs, nkw, nvw, ngt, nsa_pe_k[l], nsa_pe_v[l],
                          nsa_ck1[l], nsa_ck2[l], nsa_cv1[l], nsa_cv2[l], cos, sin)
        y_gdn = gated_deltanet(gq, gk, gv, ga, gbeta, gz, gdn_conv[l], gdn_a_log[l], gdn_dt_bias[l], gdn_norm[l])
        y_hgrn = hgrn2(hq, hf, hi, hg, lb_all[l], hgrn_norm[l])
        y = jnp.concatenate([y_nsa.astype(x.dtype), y_gdn.astype(x.dtype), y_hgrn.astype(x.dtype)], axis=-1)
        x = x + y @ w_out[l]
        x = x + 0.5 * swiglu(rms_norm(x, ffn2_norm[l]), ffn2_gate[l], ffn2_up[l], ffn2_down[l])
    return rms_norm(x, final_norm)
```

```python
import functools

import jax
import jax.numpy as jnp
import numpy as np
from jax import lax
from jax.experimental import pallas as pl
from jax.experimental.pallas import tpu as pltpu

F32 = jnp.float32
BF16 = jnp.bfloat16
HIGHEST = lax.Precision.HIGHEST

D_MODEL = 2048
HEAD_DIM = 128
ROPE_THETA = 10000.0
EPS = 1e-6
MASK_VALUE = -1e30
FORCE_VALUE = 1e9
MIN_FORGET = 1e-6
NSA_HEADS = 6
NSA_KV_HEADS = 2
NSA_REP = NSA_HEADS // NSA_KV_HEADS
CMP_BLOCK = 32
CMP_STRIDE = 16
SEL_BLOCK = 64
N_SEL = 16
WINDOW = 512
Q_BLOCK = 128
GDN_HEADS = 5
CONV_WIDTH = 4
GDN_CHUNK = 64
HGRN_HEADS = 5
HGRN_CHUNK = 64
D_FF = 5632

NSA_WIDTH = NSA_HEADS * HEAD_DIM
KV_WIDTH = NSA_KV_HEADS * HEAD_DIM
GDN_WIDTH = GDN_HEADS * HEAD_DIM
HGRN_WIDTH = HGRN_HEADS * HEAD_DIM
MIX_WIDTH = NSA_WIDTH + GDN_WIDTH + HGRN_WIDTH
IN_SIZES = (NSA_WIDTH, KV_WIDTH, KV_WIDTH, KV_WIDTH, KV_WIDTH, KV_WIDTH, KV_WIDTH, 3 * NSA_HEADS,
            GDN_WIDTH, GDN_WIDTH, GDN_WIDTH, GDN_HEADS, GDN_HEADS, GDN_WIDTH,
            HGRN_WIDTH, HGRN_WIDTH, HGRN_WIDTH, HGRN_WIDTH)

LANE = 128
VMEM_LIMIT = 48 * 1024 * 1024

CB_GQ = 0
CB_GK = 5
CB_GV = 10
CB_GZ = 15
CB_HQ = 20
CB_HF = 25
CB_HI = 30
CB_HG = 35
CB_NKC = 40
CB_NVC = 42
CB_NKS = 44
CB_NVS = 46
CB_NKW = 48
CB_NVW = 50
CB_NGT = 52
CB_GAB = 53
CB_NQ = 54
P_BLOCKS = 60
P_WIDTH = P_BLOCKS * LANE


def _cparams(*sem):
    return pltpu.CompilerParams(dimension_semantics=sem, vmem_limit_bytes=VMEM_LIMIT)


def _dot(a, b, dims=((1,), (0,)), precision=None):
    return lax.dot_general(a, b, (dims, ((), ())), precision=precision, preferred_element_type=F32)


def _dot_nt(a, b, precision=None):
    return _dot(a, b, ((1,), (1,)), precision)


def _dot_tn(a, b, precision=None):
    return _dot(a, b, ((0,), (0,)), precision)


def _rmsnorm_kernel(x_ref, g_ref, o_ref):
    x = x_ref[...]
    y = x * lax.rsqrt(jnp.mean(x * x, axis=-1, keepdims=True) + EPS)
    o_ref[...] = (y * g_ref[...]).astype(o_ref.dtype)


def _rmsnorm(x, g, out_dtype, tm=512):
    t, d = x.shape
    return pl.pallas_call(
        _rmsnorm_kernel,
        grid=(t // tm,),
        in_specs=[pl.BlockSpec((tm, d), lambda i: (i, 0)), pl.BlockSpec((1, d), lambda i: (0, 0))],
        out_specs=pl.BlockSpec((tm, d), lambda i: (i, 0)),
        out_shape=jax.ShapeDtypeStruct((t, d), out_dtype),
        compiler_params=_cparams("parallel"),
        name="rmsnorm",
    )(x, g.reshape(1, d))


def _ffn_up_kernel(h_ref, wg_ref, wu_ref, o_ref):
    h = h_ref[...]
    a = jnp.dot(h, wg_ref[...], preferred_element_type=F32)
    b = jnp.dot(h, wu_ref[...], preferred_element_type=F32)
    o_ref[...] = (a * jax.nn.sigmoid(a) * b).astype(o_ref.dtype)


def _ffn_up(h, wg, wu, tm=1024, tn=512):
    t, k = h.shape
    n = wg.shape[1]
    return pl.pallas_call(
        _ffn_up_kernel,
        grid=(t // tm, n // tn),
        in_specs=[pl.BlockSpec((tm, k), lambda i, j: (i, 0)),
                  pl.BlockSpec((k, tn), lambda i, j: (0, j)),
                  pl.BlockSpec((k, tn), lambda i, j: (0, j))],
        out_specs=pl.BlockSpec((tm, tn), lambda i, j: (i, j)),
        out_shape=jax.ShapeDtypeStruct((t, n), BF16),
        compiler_params=_cparams("parallel", "parallel"),
        name="ffn_up",
    )(h, wg, wu)


def _mm_kernel(a_ref, b_ref, o_ref):
    o_ref[...] = jnp.dot(a_ref[...], b_ref[...], preferred_element_type=F32).astype(o_ref.dtype)


def _mm(a, b, out_dtype, tm=1024, tn=512):
    t, k = a.shape
    n = b.shape[1]
    return pl.pallas_call(
        _mm_kernel,
        grid=(t // tm, n // tn),
        in_specs=[pl.BlockSpec((tm, k), lambda i, j: (i, 0)), pl.BlockSpec((k, tn), lambda i, j: (0, j))],
        out_specs=pl.BlockSpec((tm, tn), lambda i, j: (i, j)),
        out_shape=jax.ShapeDtypeStruct((t, n), out_dtype),
        compiler_params=_cparams("parallel", "parallel"),
        name="proj_in",
    )(a, b)


def _mm_res_kernel(a_ref, b_ref, r_ref, o_ref, acc_ref, *, nk, scale):
    k = pl.program_id(2)

    @pl.when(k == 0)
    def _():
        acc_ref[...] = jnp.zeros_like(acc_ref)

    acc_ref[...] += jnp.dot(a_ref[...], b_ref[...], preferred_element_type=F32)

    @pl.when(k == nk - 1)
    def _():
        o_ref[...] = r_ref[...] + scale * acc_ref[...]


def _mm_res(a, b, res, scale, tm=1024, tn=1024, tk=512):
    t, kdim = a.shape
    n = b.shape[1]
    tk = min(tk, kdim)
    nk = kdim // tk
    return pl.pallas_call(
        functools.partial(_mm_res_kernel, nk=nk, scale=scale),
        grid=(t // tm, n // tn, nk),
        in_specs=[pl.BlockSpec((tm, tk), lambda i, j, k: (i, k)),
                  pl.BlockSpec((tk, tn), lambda i, j, k: (k, j)),
                  pl.BlockSpec((tm, tn), lambda i, j, k: (i, j))],
        out_specs=pl.BlockSpec((tm, tn), lambda i, j, k: (i, j)),
        out_shape=jax.ShapeDtypeStruct((t, n), F32),
        scratch_shapes=[pltpu.VMEM((tm, tn), F32)],
        compiler_params=_cparams("parallel", "parallel", "arbitrary"),
        name="mm_residual",
    )(a, b, res)


def _rope(x, cos2, sin2):
    return x * cos2 + pltpu.roll(x, HEAD_DIM // 2, axis=1) * sin2


def _nsa_prep_kernel(kc_ref, vc_ref, ks_ref, vs_ref, kw_ref, vw_ref, cos_ref, sin_ref, cv_ref, kv_ref):
    cos2 = cos_ref[...]
    sin2 = sin_ref[...]
    for g in range(NSA_KV_HEADS):
        sl = slice(g * HEAD_DIM, (g + 1) * HEAD_DIM)
        cv_ref[g] = _rope(kc_ref[:, sl], cos2, sin2)
        cv_ref[NSA_KV_HEADS + g] = vc_ref[:, sl]
        for j, (ref, roped) in enumerate(((ks_ref, True), (vs_ref, False), (kw_ref, True), (vw_ref, False))):
            x = ref[:, sl]
            if roped:
                x = _rope(x, cos2, sin2)
            c0 = (j * NSA_KV_HEADS + g) * HEAD_DIM
            kv_ref[:, c0:c0 + HEAD_DIM] = x.astype(BF16)


def _nsa_prep(p, cos2, sin2, seq, tile=512):
    t = p.shape[0]
    per_seq = seq // tile
    kvw = KV_WIDTH

    def pspec(cb):
        return pl.BlockSpec((tile, kvw), lambda i, cb=cb: (i, cb * LANE // kvw))

    tab = pl.BlockSpec((tile, HEAD_DIM), lambda i: (i % per_seq, 0))
    return pl.pallas_call(
        _nsa_prep_kernel,
        grid=(t // tile,),
        in_specs=[pspec(CB_NKC), pspec(CB_NVC), pspec(CB_NKS), pspec(CB_NVS), pspec(CB_NKW), pspec(CB_NVW), tab, tab],
        out_specs=[pl.BlockSpec((2 * NSA_KV_HEADS, tile, HEAD_DIM), lambda i: (0, i, 0)),
                   pl.BlockSpec((tile, 4 * kvw), lambda i: (i, 0))],
        out_shape=[jax.ShapeDtypeStruct((2 * NSA_KV_HEADS, t, HEAD_DIM), F32),
                   jax.ShapeDtypeStruct((t, 4 * kvw), BF16)],
        compiler_params=_cparams("parallel"),
        name="nsa_prep",
    )(p, p, p, p, p, p, cos2, sin2)


def _nsa_compress_kernel(t_ref, pe_ref, w1_ref, w2_ref, o_ref):
    x = t_ref[...]
    half = CMP_STRIDE * HEAD_DIM
    nr = x.shape[0]
    u = _dot(x + pe_ref[0:1, :], w1_ref[0:half, :], precision=HIGHEST)
    v = _dot(x + pe_ref[1:2, :], w1_ref[half:2 * half, :], precision=HIGHEST)
    pre = u + pltpu.roll(v, nr - 1, axis=0)
    o_ref[...] = _dot(jax.nn.gelu(pre), w2_ref[...], precision=HIGHEST)


def _nsa_compress(cv, pe, w1, w2, batch, seq):
    nr = seq // CMP_STRIDE
    half = CMP_STRIDE * HEAD_DIM
    t2 = cv.reshape(2 * NSA_KV_HEADS, batch, nr, half)
    return pl.pallas_call(
        _nsa_compress_kernel,
        grid=(2 * NSA_KV_HEADS, batch),
        in_specs=[pl.BlockSpec((None, None, nr, half), lambda c, b: (c, b, 0, 0)),
                  pl.BlockSpec((None, 2, half), lambda c, b: (c // NSA_KV_HEADS, 0, 0)),
                  pl.BlockSpec((None, 2 * half, HEAD_DIM), lambda c, b: (c // NSA_KV_HEADS, 0, 0)),
                  pl.BlockSpec((None, HEAD_DIM, HEAD_DIM), lambda c, b: (c // NSA_KV_HEADS, 0, 0))],
        out_specs=pl.BlockSpec((None, None, nr, HEAD_DIM), lambda c, b: (c, b, 0, 0)),
        out_shape=jax.ShapeDtypeStruct((2 * NSA_KV_HEADS, batch, nr, HEAD_DIM), F32),
        compiler_params=_cparams("parallel", "parallel"),
        name="nsa_compress",
    )(t2, pe.reshape(2, 2, half), w1, w2)


SEL_TILE = 512


def _masked_softmax(s, mask):
    s = jnp.where(mask, s, MASK_VALUE)
    m = jnp.max(s, axis=-1, keepdims=True)
    e = jnp.where(mask, jnp.exp(s - m), 0.0)
    return e / jnp.maximum(jnp.sum(e, axis=-1, keepdims=True), 1e-30)


def _nsa_attn_kernel(q_ref, gt_ref, cos_ref, sin_ref, kc_ref, vc_ref, ks_ref, vs_ref, kw_ref, vw_ref,
                     o_ref, m_ref, l_ref, acc_ref, *, seq):
    g = pl.program_id(1)
    i = pl.program_id(2)
    qn, r_n, d = Q_BLOCK, NSA_REP, HEAD_DIM
    rows = r_n * qn
    n_blk = seq // SEL_BLOCK
    n_sel = min(N_SEL, n_blk)
    ncr = seq // CMP_STRIDE

    cos2 = cos_ref[...]
    sin2 = sin_ref[...]
    qb = q_ref[...]
    q = jnp.concatenate([_rope(qb[:, r * d:(r + 1) * d], cos2, sin2) * (HEAD_DIM ** -0.5) for r in range(r_n)],
                        axis=0)
    q_bf = q.astype(BF16)
    t0 = i * qn
    t_col = t0 + lax.broadcasted_iota(jnp.int32, (qn, 1), 0)
    t_rows = jnp.concatenate([t_col] * r_n, axis=0)

    kc = kc_ref[...]
    vc = vc_ref[...]
    s_c = _dot_nt(q, kc, precision=HIGHEST)
    cmp_end = lax.broadcasted_iota(jnp.int32, (1, ncr), 1) * CMP_STRIDE + (CMP_BLOCK - 1)
    p_c = _masked_softmax(s_c, cmp_end <= t_rows)
    o_c = _dot(p_c, vc, precision=HIGHEST)

    pc_sum = p_c[0:qn]
    for r in range(1, r_n):
        pc_sum = pc_sum + p_c[r * qn:(r + 1) * qn]
    m_i = lax.broadcasted_iota(jnp.int32, (n_blk, ncr), 0)
    c_i = lax.broadcasted_iota(jnp.int32, (n_blk, ncr), 1)
    ov = (jnp.minimum(c_i * CMP_STRIDE + CMP_BLOCK, m_i * SEL_BLOCK + SEL_BLOCK)
          - jnp.maximum(c_i * CMP_STRIDE, m_i * SEL_BLOCK))
    ov_t = jnp.where(c_i < ncr - 1, jnp.maximum(ov, 0).astype(F32) / CMP_BLOCK, 0.0)
    imp = _dot_nt(ov_t, pc_sum, precision=HIGHEST)
    blk = lax.broadcasted_iota(jnp.int32, (n_blk, qn), 0)
    t_row = t0 + lax.broadcasted_iota(jnp.int32, (n_blk, qn), 1)
    cur = t_row // SEL_BLOCK
    forced = (blk == 0) | (blk == cur) | (blk == cur - 1)
    future = blk * SEL_BLOCK > t_row
    imp = jnp.where(forced, FORCE_VALUE, jnp.where(future, -FORCE_VALUE, imp))
    rank = jnp.zeros((n_blk, qn), F32)
    for j in range(n_blk):
        row = imp[j:j + 1, :]
        beats = (row > imp) | ((row == imp) & (blk > j))
        rank = rank + jnp.where(beats, 1.0, 0.0)
    sel_t = jnp.where(rank < n_sel, 1.0, 0.0)
    sel = sel_t.T.astype(BF16)

    m_ref[...] = jnp.full((rows, 1), MASK_VALUE, F32)
    l_ref[...] = jnp.zeros((rows, 1), F32)
    acc_ref[...] = jnp.zeros((rows, d), F32)
    n_tiles = (t0 + qn + SEL_TILE - 1) // SEL_TILE

    def sel_step(kt, carry):
        k0 = pl.multiple_of(kt * SEL_TILE, SEL_TILE)
        k = ks_ref[pl.ds(k0, SEL_TILE), :]
        v = vs_ref[pl.ds(k0, SEL_TILE), :]
        s = _dot_nt(q_bf, k)
        kpos = k0 + lax.broadcasted_iota(jnp.int32, (1, SEL_TILE), 1)
        e_blk = lax.broadcasted_iota(jnp.int32, (n_blk, SEL_TILE), 0)
        e_key = k0 + lax.broadcasted_iota(jnp.int32, (n_blk, SEL_TILE), 1)
        expand = jnp.where(e_blk == e_key // SEL_BLOCK, 1.0, 0.0).astype(BF16)
        chosen = _dot(sel, expand)
        ok_q = (chosen > 0.5) & (kpos <= t_col)
        ok = jnp.concatenate([ok_q] * r_n, axis=0)
        s = jnp.where(ok, s, MASK_VALUE)
        m_old = m_ref[...]
        m_new = jnp.maximum(m_old, jnp.max(s, axis=-1, keepdims=True))
        alpha = jnp.exp(m_old - m_new)
        p = jnp.where(ok, jnp.exp(s - m_new), 0.0)
        l_ref[...] = alpha * l_ref[...] + jnp.sum(p, axis=-1, keepdims=True)
        acc_ref[...] = alpha * acc_ref[...] + _dot(p.astype(BF16), v)
        m_ref[...] = m_new
        return carry

    lax.fori_loop(0, n_tiles, sel_step, 0)
    o_s = acc_ref[...] / jnp.maximum(l_ref[...], 1e-30)

    wlen = WINDOW + qn
    w0 = pl.multiple_of(jnp.maximum(t0 - WINDOW, 0), qn)
    kw = kw_ref[pl.ds(w0, wlen), :]
    vw = vw_ref[pl.ds(w0, wlen), :]
    s_w = _dot_nt(q_bf, kw)
    wpos = w0 + lax.broadcasted_iota(jnp.int32, (1, wlen), 1)
    p_w = _masked_softmax(s_w, (wpos <= t_rows) & (wpos > t_rows - WINDOW))
    o_w = _dot(p_w.astype(BF16), vw)

    gates = jax.nn.sigmoid(gt_ref[...])
    outs = []
    for r in range(r_n):
        sl = slice(r * qn, (r + 1) * qn)
        o_r = 0.0
        for c, o_b in enumerate((o_c, o_s, o_w)):
            lane = r * 3 + c
            gcol = jnp.where(g == 0, gates[:, lane:lane + 1], gates[:, 3 * r_n + lane:3 * r_n + lane + 1])
            o_r = o_r + gcol * o_b[sl]
        outs.append(o_r)
    o_ref[...] = jnp.concatenate(outs, axis=1).astype(o_ref.dtype)


def _nsa_attn(p, cmp_kv, kvbf, cos2, sin2, batch, seq):
    t = p.shape[0]
    n_qb = seq // Q_BLOCK
    ncr = seq // CMP_STRIDE
    gw = NSA_REP * HEAD_DIM
    rows = NSA_REP * Q_BLOCK

    def kv_spec(j):
        return pl.BlockSpec((seq, HEAD_DIM), lambda b, g, i, j=j: (b, j * NSA_KV_HEADS + g))

    return pl.pallas_call(
        functools.partial(_nsa_attn_kernel, seq=seq),
        grid=(batch, NSA_KV_HEADS, n_qb),
        in_specs=[pl.BlockSpec((Q_BLOCK, gw), lambda b, g, i: (b * n_qb + i, CB_NQ // NSA_REP + g)),
                  pl.BlockSpec((Q_BLOCK, LANE), lambda b, g, i: (b * n_qb + i, CB_NGT)),
                  pl.BlockSpec((Q_BLOCK, HEAD_DIM), lambda b, g, i: (i, 0)),
                  pl.BlockSpec((Q_BLOCK, HEAD_DIM), lambda b, g, i: (i, 0)),
                  pl.BlockSpec((None, None, ncr, HEAD_DIM), lambda b, g, i: (g, b, 0, 0)),
                  pl.BlockSpec((None, None, ncr, HEAD_DIM), lambda b, g, i: (NSA_KV_HEADS + g, b, 0, 0)),
                  kv_spec(0), kv_spec(1), kv_spec(2), kv_spec(3)],
        out_specs=pl.BlockSpec((Q_BLOCK, gw), lambda b, g, i: (b * n_qb + i, g)),
        out_shape=jax.ShapeDtypeStruct((t, NSA_WIDTH), BF16),
        scratch_shapes=[pltpu.VMEM((rows, 1), F32), pltpu.VMEM((rows, 1), F32), pltpu.VMEM((rows, HEAD_DIM), F32)],
        compiler_params=_cparams("parallel", "parallel", "arbitrary"),
        name="nsa_attn",
    )(p, p, cos2, sin2, cmp_kv, cmp_kv, kvbf, kvbf, kvbf, kvbf)


def _tri(n, lower_inclusive=True):
    r = lax.broadcasted_iota(jnp.int32, (n, n), 0)
    c = lax.broadcasted_iota(jnp.int32, (n, n), 1)
    return r, c


def _softplus(x):
    return jnp.maximum(x, 0.0) + jnp.log1p(jnp.exp(-jnp.abs(x)))


def _gated_rmsnorm(o, norm_w, gate):
    y = o * lax.rsqrt(jnp.mean(o * o, axis=-1, keepdims=True) + EPS)
    return (y * norm_w) * (gate * jax.nn.sigmoid(gate))


def _gdn_kernel(q_ref, k_ref, v_ref, ab_ref, z_ref, wconv_ref, alog_ref, dtb_ref, nw_ref, o_ref,
                tail_ref, state_ref):
    c_n, d, h_n = GDN_CHUNK, HEAD_DIM, GDN_HEADS
    w = h_n * d
    ci = pl.program_id(1)

    @pl.when(ci == 0)
    def _():
        tail_ref[...] = jnp.zeros_like(tail_ref)
        state_ref[...] = jnp.zeros_like(state_ref)

    x = jnp.concatenate([q_ref[...], k_ref[...], v_ref[...]], axis=1)
    xcat = jnp.concatenate([tail_ref[...], x], axis=0)
    wc = wconv_ref[...]
    conv = None
    for j in range(CONV_WIDTH):
        shift = CONV_WIDTH - 1 - j
        xs = xcat if shift == 0 else pltpu.roll(xcat, shift, axis=0)
        term = xs[8:8 + c_n] * wc[j:j + 1, :]
        conv = term if conv is None else conv + term
    tail_ref[...] = x[c_n - 8:c_n]
    act = conv * jax.nn.sigmoid(conv)

    ab = ab_ref[...]
    g_all = -jnp.exp(alog_ref[...]) * _softplus(ab + dtb_ref[...])
    beta_all = jax.nn.sigmoid(ab)
    r_i, c_i = _tri(c_n)
    lower = jnp.where(r_i >= c_i, 1.0, 0.0)
    gc_all = _dot(lower, g_all, precision=HIGHEST)
    gct_all = _dot_tn(g_all, jnp.where(r_i <= c_i, 1.0, 0.0), precision=HIGHEST)
    eye = jnp.where(r_i == c_i, 1.0, 0.0)
    causal = r_i >= c_i
    strict = r_i > c_i
    nw = nw_ref[...]

    outs = []
    for h in range(h_n):
        sl = slice(h * d, (h + 1) * d)
        xq = act[:, h * d:(h + 1) * d]
        xk = act[:, w + h * d:w + (h + 1) * d]
        v = act[:, 2 * w + h * d:2 * w + (h + 1) * d]
        q = xq * lax.rsqrt(jnp.sum(xq * xq, axis=-1, keepdims=True) + EPS) * (HEAD_DIM ** -0.5)
        k = xk * lax.rsqrt(jnp.sum(xk * xk, axis=-1, keepdims=True) + EPS)
        beta = beta_all[:, h_n + h:h_n + h + 1]
        gc = gc_all[:, h:h + 1]
        gc_row = gct_all[h:h + 1, :]
        gc_last = gc_all[c_n - 1:c_n, h:h + 1]
        decay = jnp.where(causal, jnp.exp(jnp.where(causal, gc - gc_row, 0.0)), 0.0)
        kb = k * beta
        a_mat = jnp.where(strict, _dot_nt(kb, k, precision=HIGHEST) * decay, 0.0)
        egc = jnp.exp(gc)
        rhs = jnp.concatenate([v * beta, kb * egc], axis=1)
        pw = -a_mat
        t_inv = eye + pw
        span = 2
        while span < c_n:
            pw = _dot(pw, pw, precision=HIGHEST)
            t_inv = t_inv + _dot(t_inv, pw, precision=HIGHEST)
            span *= 2
        sol = _dot(t_inv, rhs, precision=HIGHEST)
        u = sol[:, :d]
        wm = sol[:, d:]
        attn = _dot_nt(q, k, precision=HIGHEST) * decay
        state = state_ref[h]
        v_new = u - _dot(wm, state, precision=HIGHEST)
        o = _dot(q * egc, state, precision=HIGHEST) + _dot(attn, v_new, precision=HIGHEST)
        k_dec = k * jnp.exp(gc_last - gc)
        state_ref[h] = state * jnp.exp(gc_last) + _dot_tn(k_dec, v_new, precision=HIGHEST)
        outs.append(_gated_rmsnorm(o, nw, z_ref[:, sl]))
    o_ref[...] = jnp.concatenate(outs, axis=1).astype(o_ref.dtype)


def _gdn(p, w_conv, a_log, dt_bias, norm_w, batch, seq):
    t = p.shape[0]
    c_n, h_n, d = GDN_CHUNK, GDN_HEADS, HEAD_DIM
    w = h_n * d
    n_c = seq // c_n
    pad = jnp.zeros((LANE - h_n,), F32)
    alog = jnp.concatenate([a_log, pad]).reshape(1, LANE)
    dtb = jnp.concatenate([dt_bias, pad]).reshape(1, LANE)

    def pspec(cb, width):
        return pl.BlockSpec((c_n, width), lambda b, c, cb=cb, width=width: (b * n_c + c, cb * LANE // width))

    def whole(shape):
        return pl.BlockSpec(shape, lambda b, c: (0,) * len(shape))

    return pl.pallas_call(
        _gdn_kernel,
        grid=(batch, n_c),
        in_specs=[pspec(CB_GQ, w), pspec(CB_GK, w), pspec(CB_GV, w), pspec(CB_GAB, LANE), pspec(CB_GZ, w),
                  whole((CONV_WIDTH, 3 * w)), whole((1, LANE)), whole((1, LANE)), whole((1, d))],
        out_specs=pl.BlockSpec((c_n, w), lambda b, c: (b * n_c + c, 0)),
        out_shape=jax.ShapeDtypeStruct((t, w), BF16),
        scratch_shapes=[pltpu.VMEM((8, 3 * w), F32), pltpu.VMEM((h_n, d, d), F32)],
        compiler_params=_cparams("parallel", "arbitrary"),
        name="gdn",
    )(p, p, p, p, p, w_conv, alog, dtb, norm_w.reshape(1, d))


HGRN_SUB = 16


def _hgrn_kernel(q_ref, f_ref, i_ref, g_ref, lb_ref, nw_ref, o_ref, state_ref):
    c_n, d, h_n, sb = HGRN_CHUNK, HEAD_DIM, HGRN_HEADS, HGRN_SUB
    ci = pl.program_id(1)

    @pl.when(ci == 0)
    def _():
        state_ref[...] = jnp.zeros_like(state_ref)

    r_i, c_i = _tri(c_n)
    lower = jnp.where(r_i >= c_i, 1.0, 0.0)
    lane_i = lax.broadcasted_iota(jnp.int32, (sb, c_n), 1)
    row_i = lax.broadcasted_iota(jnp.int32, (sb, c_n), 0)
    nw = nw_ref[...]
    outs = []
    for h in range(h_n):
        sl = slice(h * d, (h + 1) * d)
        lb = lb_ref[:, sl]
        f_gate = lb + (1.0 - lb) * jax.nn.sigmoid(f_ref[:, sl])
        log_f = jnp.log(jnp.maximum(f_gate, MIN_FORGET))
        k = 1.0 - f_gate
        q = q_ref[:, sl]
        v = i_ref[:, sl]
        bcum = _dot(lower, log_f, precision=HIGHEST)
        a_rows = []
        for sbi in range(c_n // sb):
            r0 = sbi * sb
            q_s = q[r0:r0 + sb]
            b_s = bcum[r0:r0 + sb]
            a_blk = jnp.zeros((sb, c_n), F32)
            if sbi > 0:
                b_ref0 = bcum[r0:r0 + 1]
                q_t = q_s * jnp.exp(b_s - b_ref0)
                k_t = k * jnp.exp(jnp.minimum(b_ref0 - bcum, 0.0))
                a_blk = jnp.where(lane_i < r0, _dot_nt(q_t, k_t, precision=HIGHEST), 0.0)
            for jj in range(sb):
                j = r0 + jj
                ok = lax.broadcasted_iota(jnp.int32, (sb, 1), 0) >= jj
                e = jnp.where(ok, jnp.exp(jnp.where(ok, b_s - bcum[j:j + 1], 0.0)), 0.0)
                col = jnp.sum(q_s * k[j:j + 1] * e, axis=-1, keepdims=True)
                a_blk = jnp.where(lane_i == j, col, a_blk)
            a_rows.append(a_blk)
        a_mat = jnp.concatenate(a_rows, axis=0)
        state_t = state_ref[h]
        o = _dot_nt(q * jnp.exp(bcum), state_t, precision=HIGHEST) + _dot(a_mat, v, precision=HIGHEST)
        b_last = bcum[c_n - 1:c_n]
        k_dec = k * jnp.exp(b_last - bcum)
        state_ref[h] = state_t * jnp.exp(b_last) + _dot_tn(v, k_dec, precision=HIGHEST)
        outs.append(_gated_rmsnorm(o, nw, g_ref[:, sl]))
    o_ref[...] = jnp.concatenate(outs, axis=1).astype(o_ref.dtype)


def _hgrn(p, lb, norm_w, batch, seq):
    t = p.shape[0]
    c_n, h_n, d = HGRN_CHUNK, HGRN_HEADS, HEAD_DIM
    w = h_n * d
    n_c = seq // c_n

    def pspec(cb):
        return pl.BlockSpec((c_n, w), lambda b, c, cb=cb: (b * n_c + c, cb * LANE // w))

    return pl.pallas_call(
        _hgrn_kernel,
        grid=(batch, n_c),
        in_specs=[pspec(CB_HQ), pspec(CB_HF), pspec(CB_HI), pspec(CB_HG),
                  pl.BlockSpec((1, w), lambda b, c: (0, 0)), pl.BlockSpec((1, d), lambda b, c: (0, 0))],
        out_specs=pl.BlockSpec((c_n, w), lambda b, c: (b * n_c + c, 0)),
        out_shape=jax.ShapeDtypeStruct((t, w), BF16),
        scratch_shapes=[pltpu.VMEM((h_n, d, d), F32)],
        compiler_params=_cparams("parallel", "arbitrary"),
        name="hgrn",
    )(p, p, p, p, lb.reshape(1, w), norm_w.reshape(1, d))


def _pack_w_in(w_in):
    offs = np.concatenate([[0], np.cumsum(np.array(IN_SIZES))])
    seg = [w_in[:, int(offs[j]):int(offs[j + 1])] for j in range(len(IN_SIZES))]
    k = w_in.shape[0]

    def padded(parts):
        wdt = sum(x.shape[1] for x in parts)
        return parts + [jnp.zeros((k, LANE - wdt), w_in.dtype)]

    cols = (seg[8:11] + seg[13:18] + seg[1:7] + padded([seg[7]]) + padded([seg[11], seg[12]]) + [seg[0]])
    return jnp.concatenate(cols, axis=1)


def _rope_tables(seq):
    inv = 1.0 / (ROPE_THETA ** (jnp.arange(0, HEAD_DIM, 2, dtype=F32) / HEAD_DIM))
    ang = jnp.arange(seq, dtype=F32)[:, None] * inv[None, :]
    cos, sin = jnp.cos(ang), jnp.sin(ang)
    return jnp.concatenate([cos, cos], axis=1), jnp.concatenate([-sin, sin], axis=1)


def _mixers(p, batch, seq, layer_params):
    (pe_k, pe_v, ck1, ck2, cv1, cv2, conv_w, a_log, dt_bias, gdn_norm, lb, hgrn_norm, cos2, sin2) = layer_params
    cv, kvbf = _nsa_prep(p, cos2, sin2, seq)
    cmp_kv = _nsa_compress(cv, jnp.stack([pe_k, pe_v]), jnp.stack([ck1, cv1]), jnp.stack([ck2, cv2]), batch, seq)
    y_nsa = _nsa_attn(p, cmp_kv, kvbf, cos2, sin2, batch, seq)
    y_gdn = _gdn(p, conv_w, a_log, dt_bias, gdn_norm, batch, seq)
    y_hgrn = _hgrn(p, lb, hgrn_norm, batch, seq)
    return jnp.concatenate([y_nsa, y_gdn, y_hgrn], axis=1)


def kernel(x, ffn1_norm, ffn1_gate, ffn1_up, ffn1_down, mix_norm, w_in, w_out, nsa_pe_k, nsa_pe_v, nsa_ck1, nsa_ck2, nsa_cv1, nsa_cv2, gdn_conv, gdn_a_log, gdn_dt_bias, gdn_norm, hgrn_lb, hgrn_norm, ffn2_norm, ffn2_gate, ffn2_up, ffn2_down, final_norm):
    batch, seq, dm = x.shape
    depth = w_in.shape[0]
    cos2, sin2 = _rope_tables(seq)
    p_lb = jax.nn.softmax(hgrn_lb.astype(F32), axis=0)
    lb_all = jnp.cumsum(p_lb, axis=0) - p_lb[0:1]
    xt = x.reshape(batch * seq, dm)

    def ffn(xt, norm, wg, wu, wd):
        h = _rmsnorm(xt, norm, BF16)
        act = _ffn_up(h, wg.astype(BF16), wu.astype(BF16))
        return _mm_res(act, wd.astype(BF16), xt, 0.5)

    for l in range(depth):
        xt = ffn(xt, ffn1_norm[l], ffn1_gate[l], ffn1_up[l], ffn1_down[l])
        h = _rmsnorm(xt, mix_norm[l], BF16)
        p = _mm(h, _pack_w_in(w_in[l]).astype(BF16), F32)
        y = _mixers(p, batch, seq, (nsa_pe_k[l], nsa_pe_v[l], nsa_ck1[l], nsa_ck2[l], nsa_cv1[l], nsa_cv2[l],
                                    gdn_conv[l], gdn_a_log[l], gdn_dt_bias[l], gdn_norm[l], lb_all[l], hgrn_norm[l],
                                    cos2, sin2))
        xt = _mm_res(y, w_out[l].astype(BF16), xt, 1.0)
        xt = ffn(xt, ffn2_norm[l], ffn2_gate[l], ffn2_up[l], ffn2_down[l])
    return _rmsnorm(xt, final_norm, F32).reshape(batch, seq, dm)
```

```python
import functools

import jax
import jax.numpy as jnp
import numpy as np
from jax import lax
from jax.experimental import pallas as pl
from jax.experimental.pallas import tpu as pltpu

F32 = jnp.float32
BF16 = jnp.bfloat16
HIGHEST = lax.Precision.HIGHEST

D_MODEL = 2048
HEAD_DIM = 128
ROPE_THETA = 10000.0
EPS = 1e-6
MASK_VALUE = -1e30
FORCE_VALUE = 1e9
MIN_FORGET = 1e-6
NSA_HEADS = 6
NSA_KV_HEADS = 2
NSA_REP = NSA_HEADS // NSA_KV_HEADS
CMP_BLOCK = 32
CMP_STRIDE = 16
SEL_BLOCK = 64
N_SEL = 16
WINDOW = 512
Q_BLOCK = 128
GDN_HEADS = 5
CONV_WIDTH = 4
GDN_CHUNK = 64
HGRN_HEADS = 5
HGRN_CHUNK = 64
D_FF = 5632

NSA_WIDTH = NSA_HEADS * HEAD_DIM
KV_WIDTH = NSA_KV_HEADS * HEAD_DIM
GDN_WIDTH = GDN_HEADS * HEAD_DIM
HGRN_WIDTH = HGRN_HEADS * HEAD_DIM
MIX_WIDTH = NSA_WIDTH + GDN_WIDTH + HGRN_WIDTH
IN_SIZES = (NSA_WIDTH, KV_WIDTH, KV_WIDTH, KV_WIDTH, KV_WIDTH, KV_WIDTH, KV_WIDTH, 3 * NSA_HEADS,
            GDN_WIDTH, GDN_WIDTH, GDN_WIDTH, GDN_HEADS, GDN_HEADS, GDN_WIDTH,
            HGRN_WIDTH, HGRN_WIDTH, HGRN_WIDTH, HGRN_WIDTH)

LANE = 128
VMEM_LIMIT = 48 * 1024 * 1024

CB_GQ = 0
CB_GK = 5
CB_GV = 10
CB_GZ = 15
CB_HQ = 20
CB_HF = 25
CB_HI = 30
CB_HG = 35
CB_NKC = 40
CB_NVC = 42
CB_NKS = 44
CB_NVS = 46
CB_NKW = 48
CB_NVW = 50
CB_NGT = 52
CB_GAB = 53
CB_NQ = 54
P_BLOCKS = 60
P_WIDTH = P_BLOCKS * LANE


def _cparams(*sem):
    return pltpu.CompilerParams(dimension_semantics=sem, vmem_limit_bytes=VMEM_LIMIT)


def _dot(a, b, dims=((1,), (0,)), precision=None):
    return lax.dot_general(a, b, (dims, ((), ())), precision=precision, preferred_element_type=F32)


def _dot_nt(a, b, precision=None):
    return _dot(a, b, ((1,), (1,)), precision)


def _dot_tn(a, b, precision=None):
    return _dot(a, b, ((0,), (0,)), precision)


def _rmsnorm_kernel(x_ref, g_ref, o_ref):
    x = x_ref[...]
    y = x * lax.rsqrt(jnp.mean(x * x, axis=-1, keepdims=True) + EPS)
    o_ref[...] = (y * g_ref[...]).astype(o_ref.dtype)


def _rmsnorm(x, g, out_dtype, tm=512):
    t, d = x.shape
    return pl.pallas_call(
        _rmsnorm_kernel,
        grid=(t // tm,),
        in_specs=[pl.BlockSpec((tm, d), lambda i: (i, 0)), pl.BlockSpec((1, d), lambda i: (0, 0))],
        out_specs=pl.BlockSpec((tm, d), lambda i: (i, 0)),
        out_shape=jax.ShapeDtypeStruct((t, d), out_dtype),
        compiler_params=_cparams("parallel"),
        name="rmsnorm",
    )(x, g.reshape(1, d))


def _ffn_up_kernel(h_ref, wg_ref, wu_ref, o_ref):
    h = h_ref[...]
    a = jnp.dot(h, wg_ref[...], preferred_element_type=F32)
    b = jnp.dot(h, wu_ref[...], preferred_element_type=F32)
    o_ref[...] = (a * jax.nn.sigmoid(a) * b).astype(o_ref.dtype)


def _ffn_up(h, wg, wu, tm=1024, tn=512):
    t, k = h.shape
    n = wg.shape[1]
    return pl.pallas_call(
        _ffn_up_kernel,
        grid=(t // tm, n // tn),
        in_specs=[pl.BlockSpec((tm, k), lambda i, j: (i, 0)),
                  pl.BlockSpec((k, tn), lambda i, j: (0, j)),
                  pl.BlockSpec((k, tn), lambda i, j: (0, j))],
        out_specs=pl.BlockSpec((tm, tn), lambda i, j: (i, j)),
        out_shape=jax.ShapeDtypeStruct((t, n), BF16),
        compiler_params=_cparams("parallel", "parallel"),
        name="ffn_up",
    )(h, wg, wu)


def _mm_kernel(a_ref, b_ref, o_ref):
    o_ref[...] = jnp.dot(a_ref[...], b_ref[...], preferred_element_type=F32).astype(o_ref.dtype)


def _mm(a, b, out_dtype, tm=1024, tn=512):
    t, k = a.shape
    n = b.shape[1]
    return pl.pallas_call(
        _mm_kernel,
        grid=(t // tm, n // tn),
        in_specs=[pl.BlockSpec((tm, k), lambda i, j: (i, 0)), pl.BlockSpec((k, tn), lambda i, j: (0, j))],
        out_specs=pl.BlockSpec((tm, tn), lambda i, j: (i, j)),
        out_shape=jax.ShapeDtypeStruct((t, n), out_dtype),
        compiler_params=_cparams("parallel", "parallel"),
        name="proj_in",
    )(a, b)


def _mm_res_kernel(a_ref, b_ref, r_ref, o_ref, *, scale):
    o_ref[...] = r_ref[...] + scale * jnp.dot(a_ref[...], b_ref[...], preferred_element_type=F32)


def _mm_res(a, b, res, scale, tm=1024, tn=512):
    t, kdim = a.shape
    n = b.shape[1]
    return pl.pallas_call(
        functools.partial(_mm_res_kernel, scale=scale),
        grid=(t // tm, n // tn),
        in_specs=[pl.BlockSpec((tm, kdim), lambda i, j: (i, 0)),
                  pl.BlockSpec((kdim, tn), lambda i, j: (0, j)),
                  pl.BlockSpec((tm, tn), lambda i, j: (i, j))],
        out_specs=pl.BlockSpec((tm, tn), lambda i, j: (i, j)),
        out_shape=jax.ShapeDtypeStruct((t, n), F32),
        compiler_params=_cparams("parallel", "parallel"),
        name="mm_residual",
    )(a, b, res)


def _rope(x, cos2, sin2):
    return x * cos2 + pltpu.roll(x, HEAD_DIM // 2, axis=1) * sin2


def _nsa_prep_kernel(kc_ref, vc_ref, ks_ref, vs_ref, kw_ref, vw_ref, cos_ref, sin_ref, cv_ref, kv_ref):
    cos2 = cos_ref[...]
    sin2 = sin_ref[...]
    for g in range(NSA_KV_HEADS):
        sl = slice(g * HEAD_DIM, (g + 1) * HEAD_DIM)
        cv_ref[g] = _rope(kc_ref[:, sl], cos2, sin2)
        cv_ref[NSA_KV_HEADS + g] = vc_ref[:, sl]
        for j, (ref, roped) in enumerate(((ks_ref, True), (vs_ref, False), (kw_ref, True), (vw_ref, False))):
            x = ref[:, sl]
            if roped:
                x = _rope(x, cos2, sin2)
            c0 = (j * NSA_KV_HEADS + g) * HEAD_DIM
            kv_ref[:, c0:c0 + HEAD_DIM] = x.astype(BF16)


def _nsa_prep(p, cos2, sin2, seq, tile=512):
    t = p.shape[0]
    per_seq = seq // tile
    kvw = KV_WIDTH

    def pspec(cb):
        return pl.BlockSpec((tile, kvw), lambda i, cb=cb: (i, cb * LANE // kvw))

    tab = pl.BlockSpec((tile, HEAD_DIM), lambda i: (i % per_seq, 0))
    return pl.pallas_call(
        _nsa_prep_kernel,
        grid=(t // tile,),
        in_specs=[pspec(CB_NKC), pspec(CB_NVC), pspec(CB_NKS), pspec(CB_NVS), pspec(CB_NKW), pspec(CB_NVW), tab, tab],
        out_specs=[pl.BlockSpec((2 * NSA_KV_HEADS, tile, HEAD_DIM), lambda i: (0, i, 0)),
                   pl.BlockSpec((tile, 4 * kvw), lambda i: (i, 0))],
        out_shape=[jax.ShapeDtypeStruct((2 * NSA_KV_HEADS, t, HEAD_DIM), F32),
                   jax.ShapeDtypeStruct((t, 4 * kvw), BF16)],
        compiler_params=_cparams("parallel"),
        name="nsa_prep",
    )(p, p, p, p, p, p, cos2, sin2)


def _nsa_compress_kernel(t_ref, pe_ref, w1_ref, w2_ref, o_ref):
    x = t_ref[...]
    half = CMP_STRIDE * HEAD_DIM
    nr = x.shape[0]
    u = _dot(x + pe_ref[0:1, :], w1_ref[0:half, :], precision=HIGHEST)
    v = _dot(x + pe_ref[1:2, :], w1_ref[half:2 * half, :], precision=HIGHEST)
    pre = u + pltpu.roll(v, nr - 1, axis=0)
    o_ref[...] = _dot(jax.nn.gelu(pre), w2_ref[...], precision=HIGHEST)


def _nsa_compress(cv, pe, w1, w2, batch, seq):
    nr = seq // CMP_STRIDE
    half = CMP_STRIDE * HEAD_DIM
    t2 = cv.reshape(2 * NSA_KV_HEADS, batch, nr, half)
    return pl.pallas_call(
        _nsa_compress_kernel,
        grid=(2 * NSA_KV_HEADS, batch),
        in_specs=[pl.BlockSpec((None, None, nr, half), lambda c, b: (c, b, 0, 0)),
                  pl.BlockSpec((None, 2, half), lambda c, b: (c // NSA_KV_HEADS, 0, 0)),
                  pl.BlockSpec((None, 2 * half, HEAD_DIM), lambda c, b: (c // NSA_KV_HEADS, 0, 0)),
                  pl.BlockSpec((None, HEAD_DIM, HEAD_DIM), lambda c, b: (c // NSA_KV_HEADS, 0, 0))],
        out_specs=pl.BlockSpec((None, None, nr, HEAD_DIM), lambda c, b: (c, b, 0, 0)),
        out_shape=jax.ShapeDtypeStruct((2 * NSA_KV_HEADS, batch, nr, HEAD_DIM), F32),
        compiler_params=_cparams("parallel", "parallel"),
        name="nsa_compress",
    )(t2, pe.reshape(2, 2, half), w1, w2)


SEL_TILE = 512


def _masked_softmax(s, mask):
    s = jnp.where(mask, s, MASK_VALUE)
    m = jnp.max(s, axis=-1, keepdims=True)
    e = jnp.where(mask, jnp.exp(s - m), 0.0)
    return e / jnp.maximum(jnp.sum(e, axis=-1, keepdims=True), 1e-30)


def _nsa_attn_kernel(q_ref, gt_ref, cos_ref, sin_ref, kc_ref, vc_ref, ks_ref, vs_ref, kw_ref, vw_ref,
                     o_ref, m_ref, l_ref, acc_ref, *, seq):
    g = pl.program_id(1)
    i = pl.program_id(2)
    qn, r_n, d = Q_BLOCK, NSA_REP, HEAD_DIM
    rows = r_n * qn
    n_blk = seq // SEL_BLOCK
    n_sel = min(N_SEL, n_blk)
    ncr = seq // CMP_STRIDE

    cos2 = cos_ref[...]
    sin2 = sin_ref[...]
    qb = q_ref[...]
    q = jnp.concatenate([_rope(qb[:, r * d:(r + 1) * d], cos2, sin2) * (HEAD_DIM ** -0.5) for r in range(r_n)],
                        axis=0)
    q_bf = q.astype(BF16)
    t0 = i * qn
    t_col = t0 + lax.broadcasted_iota(jnp.int32, (qn, 1), 0)
    t_rows = jnp.concatenate([t_col] * r_n, axis=0)

    kc = kc_ref[...]
    vc = vc_ref[...]
    s_c = _dot_nt(q, kc, precision=HIGHEST)
    cmp_end = lax.broadcasted_iota(jnp.int32, (1, ncr), 1) * CMP_STRIDE + (CMP_BLOCK - 1)
    p_c = _masked_softmax(s_c, cmp_end <= t_rows)
    o_c = _dot(p_c, vc, precision=HIGHEST)

    pc_sum = p_c[0:qn]
    for r in range(1, r_n):
        pc_sum = pc_sum + p_c[r * qn:(r + 1) * qn]
    m_i = lax.broadcasted_iota(jnp.int32, (n_blk, ncr), 0)
    c_i = lax.broadcasted_iota(jnp.int32, (n_blk, ncr), 1)
    ov = (jnp.minimum(c_i * CMP_STRIDE + CMP_BLOCK, m_i * SEL_BLOCK + SEL_BLOCK)
          - jnp.maximum(c_i * CMP_STRIDE, m_i * SEL_BLOCK))
    ov_t = jnp.where(c_i < ncr - 1, jnp.maximum(ov, 0).astype(F32) / CMP_BLOCK, 0.0)
    imp = _dot_nt(ov_t, pc_sum, precision=HIGHEST)
    blk = lax.broadcasted_iota(jnp.int32, (n_blk, qn), 0)
    t_row = t0 + lax.broadcasted_iota(jnp.int32, (n_blk, qn), 1)
    cur = t_row // SEL_BLOCK
    forced = (blk == 0) | (blk == cur) | (blk == cur - 1)
    future = blk * SEL_BLOCK > t_row
    imp = jnp.where(forced, FORCE_VALUE, jnp.where(future, -FORCE_VALUE, imp))
    rank = jnp.zeros((n_blk, qn), F32)
    for j in range(n_blk):
        row = imp[j:j + 1, :]
        beats = (row > imp) | ((row == imp) & (blk > j))
        rank = rank + jnp.where(beats, 1.0, 0.0)
    sel_t = jnp.where(rank < n_sel, 1.0, 0.0)
    sel = sel_t.T.astype(BF16)

    m_ref[...] = jnp.full((rows, 1), MASK_VALUE, F32)
    l_ref[...] = jnp.zeros((rows, 1), F32)
    acc_ref[...] = jnp.zeros((rows, d), F32)
    n_tiles = (t0 + qn + SEL_TILE - 1) // SEL_TILE

    def sel_step(kt, carry):
        k0 = pl.multiple_of(kt * SEL_TILE, SEL_TILE)
        k = ks_ref[pl.ds(k0, SEL_TILE), :]
        v = vs_ref[pl.ds(k0, SEL_TILE), :]
        s = _dot_nt(q_bf, k)
        kpos = k0 + lax.broadcasted_iota(jnp.int32, (1, SEL_TILE), 1)
        e_blk = lax.broadcasted_iota(jnp.int32, (n_blk, SEL_TILE), 0)
        e_key = k0 + lax.broadcasted_iota(jnp.int32, (n_blk, SEL_TILE), 1)
        expand = jnp.where(e_blk == e_key // SEL_BLOCK, 1.0, 0.0).astype(BF16)
        chosen = _dot(sel, expand)
        ok_q = (chosen > 0.5) & (kpos <= t_col)
        ok = jnp.concatenate([ok_q] * r_n, axis=0)
        s = jnp.where(ok, s, MASK_VALUE)
        m_old = m_ref[...]
        m_new = jnp.maximum(m_old, jnp.max(s, axis=-1, keepdims=True))
        alpha = jnp.exp(m_old - m_new)
        p = jnp.where(ok, jnp.exp(s - m_new), 0.0)
        l_ref[...] = alpha * l_ref[...] + jnp.sum(p, axis=-1, keepdims=True)
        acc_ref[...] = alpha * acc_ref[...] + _dot(p.astype(BF16), v)
        m_ref[...] = m_new
        return carry

    lax.fori_loop(0, n_tiles, sel_step, 0)
    o_s = acc_ref[...] / jnp.maximum(l_ref[...], 1e-30)

    wlen = WINDOW + qn
    w0 = pl.multiple_of(jnp.maximum(t0 - WINDOW, 0), qn)
    kw = kw_ref[pl.ds(w0, wlen), :]
    vw = vw_ref[pl.ds(w0, wlen), :]
    s_w = _dot_nt(q_bf, kw)
    wpos = w0 + lax.broadcasted_iota(jnp.int32, (1, wlen), 1)
    p_w = _masked_softmax(s_w, (wpos <= t_rows) & (wpos > t_rows - WINDOW))
    o_w = _dot(p_w.astype(BF16), vw)

    gates = jax.nn.sigmoid(gt_ref[...])
    outs = []
    for r in range(r_n):
        sl = slice(r * qn, (r + 1) * qn)
        o_r = 0.0
        for c, o_b in enumerate((o_c, o_s, o_w)):
            lane = r * 3 + c
            gcol = jnp.where(g == 0, gates[:, lane:lane + 1], gates[:, 3 * r_n + lane:3 * r_n + lane + 1])
            o_r = o_r + gcol * o_b[sl]
        outs.append(o_r)
    o_ref[...] = jnp.concatenate(outs, axis=1).astype(o_ref.dtype)


def _nsa_attn(p, cmp_kv, kvbf, cos2, sin2, batch, seq):
    t = p.shape[0]
    n_qb = seq // Q_BLOCK
    ncr = seq // CMP_STRIDE
    gw = NSA_REP * HEAD_DIM
    rows = NSA_REP * Q_BLOCK

    def kv_spec(j):
        return pl.BlockSpec((seq, HEAD_DIM), lambda b, g, i, j=j: (b, j * NSA_KV_HEADS + g))

    return pl.pallas_call(
        functools.partial(_nsa_attn_kernel, seq=seq),
        grid=(batch, NSA_KV_HEADS, n_qb),
        in_specs=[pl.BlockSpec((Q_BLOCK, gw), lambda b, g, i: (b * n_qb + i, CB_NQ // NSA_REP + g)),
                  pl.BlockSpec((Q_BLOCK, LANE), lambda b, g, i: (b * n_qb + i, CB_NGT)),
                  pl.BlockSpec((Q_BLOCK, HEAD_DIM), lambda b, g, i: (i, 0)),
                  pl.BlockSpec((Q_BLOCK, HEAD_DIM), lambda b, g, i: (i, 0)),
                  pl.BlockSpec((None, None, ncr, HEAD_DIM), lambda b, g, i: (g, b, 0, 0)),
                  pl.BlockSpec((None, None, ncr, HEAD_DIM), lambda b, g, i: (NSA_KV_HEADS + g, b, 0, 0)),
                  kv_spec(0), kv_spec(1), kv_spec(2), kv_spec(3)],
        out_specs=pl.BlockSpec((Q_BLOCK, gw), lambda b, g, i: (b * n_qb + i, g)),
        out_shape=jax.ShapeDtypeStruct((t, NSA_WIDTH), BF16),
        scratch_shapes=[pltpu.VMEM((rows, 1), F32), pltpu.VMEM((rows, 1), F32), pltpu.VMEM((rows, HEAD_DIM), F32)],
        compiler_params=_cparams("parallel", "parallel", "arbitrary"),
        name="nsa_attn",
    )(p, p, cos2, sin2, cmp_kv, cmp_kv, kvbf, kvbf, kvbf, kvbf)


def _tri(n, lower_inclusive=True):
    r = lax.broadcasted_iota(jnp.int32, (n, n), 0)
    c = lax.broadcasted_iota(jnp.int32, (n, n), 1)
    return r, c


def _softplus(x):
    return jnp.maximum(x, 0.0) + jnp.log1p(jnp.exp(-jnp.abs(x)))


def _gated_rmsnorm(o, norm_w, gate):
    y = o * lax.rsqrt(jnp.mean(o * o, axis=-1, keepdims=True) + EPS)
    return (y * norm_w) * (gate * jax.nn.sigmoid(gate))


GDN_TILE = 256
GDN_SCAN_TILE = 128


def _split_bf16(x):
    hi = x.astype(BF16)
    return hi, (x - hi.astype(F32)).astype(BF16)


def _dot3(a, b):
    return _dot(a[0], b[0]) + (_dot(a[0], b[1]) + _dot(a[1], b[0]))


def _chunk_cumsum(x, chunk):
    pos = lax.broadcasted_iota(jnp.int32, (x.shape[0], 1), 0) % chunk
    s = 1
    while s < chunk:
        x = x + jnp.where(pos >= s, pltpu.roll(x, s, axis=0), 0.0)
        s *= 2
    return x


def _gdn_prep_kernel(q_ref, k_ref, v_ref, ab_ref, wconv_ref, alog_ref, dtb_ref,
                     u_ref, w_ref, qd_ref, att_ref, kdt_ref, dl_ref, tail_ref):
    c_n, d, h_n, tb = GDN_CHUNK, HEAD_DIM, GDN_HEADS, GDN_TILE
    nch = tb // c_n
    w = h_n * d

    @pl.when(pl.program_id(1) == 0)
    def _():
        tail_ref[...] = jnp.zeros_like(tail_ref)

    x = jnp.concatenate([q_ref[...], k_ref[...], v_ref[...]], axis=1)
    xcat = jnp.concatenate([tail_ref[...], x], axis=0)
    wc = wconv_ref[...]
    conv = None
    for j in range(CONV_WIDTH):
        shift = CONV_WIDTH - 1 - j
        xs = xcat if shift == 0 else pltpu.roll(xcat, shift, axis=0)
        term = xs[8:8 + tb] * wc[j:j + 1, :]
        conv = term if conv is None else conv + term
    tail_ref[...] = x[tb - 8:tb]
    act = conv * jax.nn.sigmoid(conv)

    ab = ab_ref[...]
    g_all = -jnp.exp(alog_ref[...]) * _softplus(ab + dtb_ref[...])
    beta_all = jax.nn.sigmoid(ab)
    gc_all = _chunk_cumsum(g_all, c_n)
    gl_all = jnp.concatenate([jnp.broadcast_to(gc_all[(c + 1) * c_n - 1:(c + 1) * c_n], (c_n, LANE))
                              for c in range(nch)], axis=0)
    gct_all = gc_all.T
    dl_ref[...] = jnp.exp(gl_all)
    r_i, c_i = _tri(tb)
    same = (r_i // c_n) == (c_i // c_n)
    causal = same & (r_i >= c_i)
    strict = same & (r_i > c_i)
    eye = jnp.where(r_i == c_i, 1.0, 0.0)
    lane_i = lax.broadcasted_iota(jnp.int32, (d, LANE), 1)

    for h in range(h_n):
        sl = slice(h * d, (h + 1) * d)
        xq = act[:, h * d:(h + 1) * d]
        xk = act[:, w + h * d:w + (h + 1) * d]
        v = act[:, 2 * w + h * d:2 * w + (h + 1) * d]
        q = xq * lax.rsqrt(jnp.sum(xq * xq, axis=-1, keepdims=True) + EPS) * (HEAD_DIM ** -0.5)
        k = xk * lax.rsqrt(jnp.sum(xk * xk, axis=-1, keepdims=True) + EPS)
        beta = beta_all[:, h_n + h:h_n + h + 1]
        gc = gc_all[:, h:h + 1]
        gc_row = gct_all[h:h + 1, :]
        decay = jnp.where(causal, jnp.exp(jnp.where(causal, gc - gc_row, 0.0)), 0.0)
        kb = k * beta
        qk = _dot_nt(jnp.concatenate([q, kb], axis=0).astype(BF16), k.astype(BF16))
        attn = qk[:tb] * decay
        a_mat = jnp.where(strict, qk[tb:] * decay, 0.0)
        egc = jnp.exp(gc)
        rhs = jnp.concatenate([v * beta, kb * egc], axis=1)
        pw = -a_mat
        t_inv = eye + pw
        pw_s = _split_bf16(pw)
        span = 2
        while span < c_n:
            pw = _dot3(pw_s, pw_s)
            pw_s = _split_bf16(pw)
            t_inv = t_inv + _dot3(_split_bf16(t_inv), pw_s)
            span *= 2
        sol = _dot3(_split_bf16(t_inv), _split_bf16(rhs))
        u_ref[:, sl] = sol[:, :d]
        w_ref[:, sl] = sol[:, d:].astype(BF16)
        qd_ref[:, sl] = (q * egc).astype(BF16)
        att_ref[:, sl] = jnp.concatenate([attn[j * LANE:(j + 1) * LANE, j * LANE:(j + 1) * LANE]
                                          for j in range(tb // LANE)], axis=0).astype(BF16)
        kdt = (k * jnp.exp(gl_all[:, h:h + 1] - gc)).T
        for c in range(nch):
            blk = kdt[:, (c // 2) * LANE:(c // 2 + 1) * LANE]
            if c % 2:
                blk = pltpu.roll(blk, c_n, axis=1)
            kdt_ref[c, h] = jnp.where(lane_i < c_n, blk, 0.0).astype(BF16)


def _gdn_scan_kernel(u_ref, w_ref, qd_ref, att_ref, kdt_ref, dl_ref, z_ref, nw_ref, o_ref, state_ref):
    c_n, d, h_n = GDN_CHUNK, HEAD_DIM, GDN_HEADS

    @pl.when(pl.program_id(1) == 0)
    def _():
        state_ref[...] = jnp.zeros_like(state_ref)

    nw = nw_ref[...]
    for c in range(GDN_SCAN_TILE // c_n):
        rs = slice(c * c_n, (c + 1) * c_n)
        for h in range(h_n):
            sl = slice(h * d, (h + 1) * d)
            state = state_ref[h]
            r = _dot(jnp.concatenate([w_ref[rs, sl], qd_ref[rs, sl]], axis=0), state.astype(BF16))
            v_new = (u_ref[rs, sl] - r[:c_n]).astype(BF16)
            vv = jnp.concatenate([v_new, v_new], axis=0)
            o = r[c_n:] + _dot(att_ref[rs, sl], vv)
            state_ref[h] = state * dl_ref[c * c_n:c * c_n + 1, h:h + 1] + _dot(kdt_ref[c, h], vv)
            o_ref[rs, sl] = _gated_rmsnorm(o, nw, z_ref[rs, sl]).astype(o_ref.dtype)


def _gdn(p, w_conv, a_log, dt_bias, norm_w, batch, seq):
    t = p.shape[0]
    c_n, h_n, d = GDN_CHUNK, GDN_HEADS, HEAD_DIM
    w = h_n * d
    tb, ts = GDN_TILE, GDN_SCAN_TILE
    pad = jnp.zeros((LANE - h_n,), F32)
    alog = jnp.concatenate([a_log, pad]).reshape(1, LANE)
    dtb = jnp.concatenate([dt_bias, pad]).reshape(1, LANE)

    def whole(shape):
        return pl.BlockSpec(shape, lambda b, c: (0,) * len(shape))

    n_t = seq // tb

    def pspec(cb, width):
        return pl.BlockSpec((tb, width), lambda b, i, cb=cb, width=width: (b * n_t + i, cb * LANE // width))

    row_w = pl.BlockSpec((tb, w), lambda b, i: (b * n_t + i, 0))
    u, wm, qd, att, kdt, dl = pl.pallas_call(
        _gdn_prep_kernel,
        grid=(batch, n_t),
        in_specs=[pspec(CB_GQ, w), pspec(CB_GK, w), pspec(CB_GV, w), pspec(CB_GAB, LANE),
                  whole((CONV_WIDTH, 3 * w)), whole((1, LANE)), whole((1, LANE))],
        out_specs=[row_w, row_w, row_w, row_w,
                   pl.BlockSpec((tb // c_n, h_n, d, LANE), lambda b, i: (b * n_t + i, 0, 0, 0)),
                   pl.BlockSpec((tb, LANE), lambda b, i: (b * n_t + i, 0))],
        out_shape=[jax.ShapeDtypeStruct((t, w), F32), jax.ShapeDtypeStruct((t, w), BF16),
                   jax.ShapeDtypeStruct((t, w), BF16), jax.ShapeDtypeStruct((t, w), BF16),
                   jax.ShapeDtypeStruct((t // c_n, h_n, d, LANE), BF16), jax.ShapeDtypeStruct((t, LANE), F32)],
        scratch_shapes=[pltpu.VMEM((8, 3 * w), F32)],
        compiler_params=_cparams("parallel", "arbitrary"),
        name="gdn_prep",
    )(p, p, p, p, w_conv, alog, dtb)

    n_s = seq // ts
    row_s = pl.BlockSpec((ts, w), lambda b, i: (b * n_s + i, 0))
    return pl.pallas_call(
        _gdn_scan_kernel,
        grid=(batch, n_s),
        in_specs=[row_s, row_s, row_s, row_s,
                  pl.BlockSpec((ts // c_n, h_n, d, LANE), lambda b, i: (b * n_s + i, 0, 0, 0)),
                  pl.BlockSpec((ts, LANE), lambda b, i: (b * n_s + i, 0)),
                  pl.BlockSpec((ts, w), lambda b, i: (b * n_s + i, CB_GZ * LANE // w)),
                  whole((1, d))],
        out_specs=row_s,
        out_shape=jax.ShapeDtypeStruct((t, w), BF16),
        scratch_shapes=[pltpu.VMEM((h_n, d, d), F32)],
        compiler_params=_cparams("parallel", "arbitrary"),
        name="gdn_scan",
    )(u, wm, qd, att, kdt, dl, p, norm_w.reshape(1, d))


HGRN_SUB = 16


def _hgrn_kernel(q_ref, f_ref, i_ref, g_ref, lb_ref, nw_ref, o_ref, state_ref):
    c_n, d, h_n, sb = HGRN_CHUNK, HEAD_DIM, HGRN_HEADS, HGRN_SUB
    ci = pl.program_id(1)

    @pl.when(ci == 0)
    def _():
        state_ref[...] = jnp.zeros_like(state_ref)

    r_i, c_i = _tri(c_n)
    lower = jnp.where(r_i >= c_i, 1.0, 0.0)
    lane_i = lax.broadcasted_iota(jnp.int32, (sb, c_n), 1)
    row_i = lax.broadcasted_iota(jnp.int32, (sb, c_n), 0)
    nw = nw_ref[...]
    outs = []
    for h in range(h_n):
        sl = slice(h * d, (h + 1) * d)
        lb = lb_ref[:, sl]
        f_gate = lb + (1.0 - lb) * jax.nn.sigmoid(f_ref[:, sl])
        log_f = jnp.log(jnp.maximum(f_gate, MIN_FORGET))
        k = 1.0 - f_gate
        q = q_ref[:, sl]
        v = i_ref[:, sl]
        bcum = _dot(lower, log_f, precision=HIGHEST)
        a_rows = []
        for sbi in range(c_n // sb):
            r0 = sbi * sb
            q_s = q[r0:r0 + sb]
            b_s = bcum[r0:r0 + sb]
            a_blk = jnp.zeros((sb, c_n), F32)
            if sbi > 0:
                b_ref0 = bcum[r0:r0 + 1]
                q_t = q_s * jnp.exp(b_s - b_ref0)
                k_t = k * jnp.exp(jnp.minimum(b_ref0 - bcum, 0.0))
                a_blk = jnp.where(lane_i < r0, _dot_nt(q_t, k_t, precision=HIGHEST), 0.0)
            for jj in range(sb):
                j = r0 + jj
                ok = lax.broadcasted_iota(jnp.int32, (sb, 1), 0) >= jj
                e = jnp.where(ok, jnp.exp(jnp.where(ok, b_s - bcum[j:j + 1], 0.0)), 0.0)
                col = jnp.sum(q_s * k[j:j + 1] * e, axis=-1, keepdims=True)
                a_blk = jnp.where(lane_i == j, col, a_blk)
            a_rows.append(a_blk)
        a_mat = jnp.concatenate(a_rows, axis=0)
        state_t = state_ref[h]
        o = _dot_nt(q * jnp.exp(bcum), state_t, precision=HIGHEST) + _dot(a_mat, v, precision=HIGHEST)
        b_last = bcum[c_n - 1:c_n]
        k_dec = k * jnp.exp(b_last - bcum)
        state_ref[h] = state_t * jnp.exp(b_last) + _dot_tn(v, k_dec, precision=HIGHEST)
        outs.append(_gated_rmsnorm(o, nw, g_ref[:, sl]))
    o_ref[...] = jnp.concatenate(outs, axis=1).astype(o_ref.dtype)


def _hgrn(p, lb, norm_w, batch, seq):
    t = p.shape[0]
    c_n, h_n, d = HGRN_CHUNK, HGRN_HEADS, HEAD_DIM
    w = h_n * d
    n_c = seq // c_n

    def pspec(cb):
        return pl.BlockSpec((c_n, w), lambda b, c, cb=cb: (b * n_c + c, cb * LANE // w))

    return pl.pallas_call(
        _hgrn_kernel,
        grid=(batch, n_c),
        in_specs=[pspec(CB_HQ), pspec(CB_HF), pspec(CB_HI), pspec(CB_HG),
                  pl.BlockSpec((1, w), lambda b, c: (0, 0)), pl.BlockSpec((1, d), lambda b, c: (0, 0))],
        out_specs=pl.BlockSpec((c_n, w), lambda b, c: (b * n_c + c, 0)),
        out_shape=jax.ShapeDtypeStruct((t, w), BF16),
        scratch_shapes=[pltpu.VMEM((h_n, d, d), F32)],
        compiler_params=_cparams("parallel", "arbitrary"),
        name="hgrn",
    )(p, p, p, p, lb.reshape(1, w), norm_w.reshape(1, d))


def _round_up(x, m):
    return -(-x // m) * m


def _pack_plan():
    offs = [int(v) for v in np.concatenate([[0], np.cumsum(np.array(IN_SIZES))])]
    plan = []
    dst = 0
    for j in (8, 9, 10, 13, 14, 15, 16, 17, 1, 2, 3, 4, 5, 6):
        plan.append((offs[j], IN_SIZES[j], dst))
        dst += IN_SIZES[j]
    plan.append((offs[7], IN_SIZES[7], CB_NGT * LANE))
    plan.append((offs[11], IN_SIZES[11] + IN_SIZES[12], CB_GAB * LANE))
    plan.append((offs[0], IN_SIZES[0], CB_NQ * LANE))
    return tuple(plan)


def _pack_kernel(w_ref, o_ref):
    for src, width, dst in _pack_plan():
        a0 = src // LANE * LANE
        off = src - a0
        span = _round_up(off + width, LANE)
        x = w_ref[:, a0:a0 + span]
        if off:
            x = pltpu.roll(x, span - off, axis=1)
        wout = _round_up(width, LANE)
        x = x[:, :wout]
        if width % LANE:
            x = jnp.where(lax.broadcasted_iota(jnp.int32, x.shape, 1) < width, x, 0.0)
        o_ref[:, dst:dst + wout] = x.astype(o_ref.dtype)


def _pack_w_in(w_in, layer, tr=256):
    _, k, n = w_in.shape
    return pl.pallas_call(
        _pack_kernel,
        grid=(k // tr,),
        in_specs=[pl.BlockSpec((None, tr, _round_up(n, LANE)), lambda i: (layer, i, 0))],
        out_specs=pl.BlockSpec((tr, P_WIDTH), lambda i: (i, 0)),
        out_shape=jax.ShapeDtypeStruct((k, P_WIDTH), BF16),
        compiler_params=_cparams("parallel"),
        name="pack_w_in",
    )(w_in)


def _rope_tables(seq):
    inv = 1.0 / (ROPE_THETA ** (jnp.arange(0, HEAD_DIM, 2, dtype=F32) / HEAD_DIM))
    ang = jnp.arange(seq, dtype=F32)[:, None] * inv[None, :]
    cos, sin = jnp.cos(ang), jnp.sin(ang)
    return jnp.concatenate([cos, cos], axis=1), jnp.concatenate([-sin, sin], axis=1)


def _mixers(p, batch, seq, layer_params):
    (pe_k, pe_v, ck1, ck2, cv1, cv2, conv_w, a_log, dt_bias, gdn_norm, lb, hgrn_norm, cos2, sin2) = layer_params
    cv, kvbf = _nsa_prep(p, cos2, sin2, seq)
    cmp_kv = _nsa_compress(cv, jnp.stack([pe_k, pe_v]), jnp.stack([ck1, cv1]), jnp.stack([ck2, cv2]), batch, seq)
    y_nsa = _nsa_attn(p, cmp_kv, kvbf, cos2, sin2, batch, seq)
    y_gdn = _gdn(p, conv_w, a_log, dt_bias, gdn_norm, batch, seq)
    y_hgrn = _hgrn(p, lb, hgrn_norm, batch, seq)
    return jnp.concatenate([y_nsa, y_gdn, y_hgrn], axis=1)


def kernel(x, ffn1_norm, ffn1_gate, ffn1_up, ffn1_down, mix_norm, w_in, w_out, nsa_pe_k, nsa_pe_v, nsa_ck1, nsa_ck2, nsa_cv1, nsa_cv2, gdn_conv, gdn_a_log, gdn_dt_bias, gdn_norm, hgrn_lb, hgrn_norm, ffn2_norm, ffn2_gate, ffn2_up, ffn2_down, final_norm):
    batch, seq, dm = x.shape
    depth = w_in.shape[0]
    cos2, sin2 = _rope_tables(seq)
    p_lb = jax.nn.softmax(hgrn_lb.astype(F32), axis=0)
    lb_all = jnp.cumsum(p_lb, axis=0) - p_lb[0:1]
    xt = x.reshape(batch * seq, dm)

    def ffn(xt, norm, wg, wu, wd):
        h = _rmsnorm(xt, norm, BF16)
        act = _ffn_up(h, wg.astype(BF16), wu.astype(BF16))
        return _mm_res(act, wd.astype(BF16), xt, 0.5)

    for l in range(depth):
        xt = ffn(xt, ffn1_norm[l], ffn1_gate[l], ffn1_up[l], ffn1_down[l])
        h = _rmsnorm(xt, mix_norm[l], BF16)
        p = _mm(h, _pack_w_in(w_in, l), F32)
        y = _mixers(p, batch, seq, (nsa_pe_k[l], nsa_pe_v[l], nsa_ck1[l], nsa_ck2[l], nsa_cv1[l], nsa_cv2[l],
                                    gdn_conv[l], gdn_a_log[l], gdn_dt_bias[l], gdn_norm[l], lb_all[l], hgrn_norm[l],
                                    cos2, sin2))
        xt = _mm_res(y, w_out[l].astype(BF16), xt, 1.0)
        xt = ffn(xt, ffn2_norm[l], ffn2_gate[l], ffn2_up[l], ffn2_down[l])
    return _rmsnorm(xt, final_norm, F32).reshape(batch, seq, dm)
```

```python
import functools

import jax
import jax.numpy as jnp
import numpy as np
from jax import lax
from jax.experimental import pallas as pl
from jax.experimental.pallas import tpu as pltpu

F32 = jnp.float32
BF16 = jnp.bfloat16
HIGHEST = lax.Precision.HIGHEST

D_MODEL = 2048
HEAD_DIM = 128
ROPE_THETA = 10000.0
EPS = 1e-6
MASK_VALUE = -1e30
FORCE_VALUE = 1e9
MIN_FORGET = 1e-6
NSA_HEADS = 6
NSA_KV_HEADS = 2
NSA_REP = NSA_HEADS // NSA_KV_HEADS
CMP_BLOCK = 32
CMP_STRIDE = 16
SEL_BLOCK = 64
N_SEL = 16
WINDOW = 512
Q_BLOCK = 128
GDN_HEADS = 5
CONV_WIDTH = 4
GDN_CHUNK = 64
HGRN_HEADS = 5
HGRN_CHUNK = 64
D_FF = 5632

NSA_WIDTH = NSA_HEADS * HEAD_DIM
KV_WIDTH = NSA_KV_HEADS * HEAD_DIM
GDN_WIDTH = GDN_HEADS * HEAD_DIM
HGRN_WIDTH = HGRN_HEADS * HEAD_DIM
MIX_WIDTH = NSA_WIDTH + GDN_WIDTH + HGRN_WIDTH
IN_SIZES = (NSA_WIDTH, KV_WIDTH, KV_WIDTH, KV_WIDTH, KV_WIDTH, KV_WIDTH, KV_WIDTH, 3 * NSA_HEADS,
            GDN_WIDTH, GDN_WIDTH, GDN_WIDTH, GDN_HEADS, GDN_HEADS, GDN_WIDTH,
            HGRN_WIDTH, HGRN_WIDTH, HGRN_WIDTH, HGRN_WIDTH)

LANE = 128
VMEM_LIMIT = 48 * 1024 * 1024

CB_GQ = 0
CB_GK = 5
CB_GV = 10
CB_GZ = 15
CB_HQ = 20
CB_HF = 25
CB_HI = 30
CB_HG = 35
CB_NKC = 40
CB_NVC = 42
CB_NKS = 44
CB_NVS = 46
CB_NKW = 48
CB_NVW = 50
CB_NGT = 52
CB_GAB = 53
CB_NQ = 54
P_BLOCKS = 60
P_WIDTH = P_BLOCKS * LANE


def _cparams(*sem):
    return pltpu.CompilerParams(dimension_semantics=sem, vmem_limit_bytes=VMEM_LIMIT)


def _dot(a, b, dims=((1,), (0,)), precision=None):
    return lax.dot_general(a, b, (dims, ((), ())), precision=precision, preferred_element_type=F32)


def _dot_nt(a, b, precision=None):
    return _dot(a, b, ((1,), (1,)), precision)


def _dot_tn(a, b, precision=None):
    return _dot(a, b, ((0,), (0,)), precision)


def _rmsnorm_kernel(x_ref, g_ref, o_ref):
    x = x_ref[...]
    y = x * lax.rsqrt(jnp.mean(x * x, axis=-1, keepdims=True) + EPS)
    o_ref[...] = (y * g_ref[...]).astype(o_ref.dtype)


def _rmsnorm(x, g, out_dtype, tm=512):
    t, d = x.shape
    return pl.pallas_call(
        _rmsnorm_kernel,
        grid=(t // tm,),
        in_specs=[pl.BlockSpec((tm, d), lambda i: (i, 0)), pl.BlockSpec((1, d), lambda i: (0, 0))],
        out_specs=pl.BlockSpec((tm, d), lambda i: (i, 0)),
        out_shape=jax.ShapeDtypeStruct((t, d), out_dtype),
        compiler_params=_cparams("parallel"),
        name="rmsnorm",
    )(x, g.reshape(1, d))


def _ffn_up_kernel(h_ref, wg_ref, wu_ref, o_ref, wg_bf, wu_bf):
    @pl.when(pl.program_id(1) == 0)
    def _():
        wg_bf[...] = wg_ref[...].astype(BF16)
        wu_bf[...] = wu_ref[...].astype(BF16)

    h = h_ref[...]
    a = jnp.dot(h, wg_bf[...], preferred_element_type=F32)
    b = jnp.dot(h, wu_bf[...], preferred_element_type=F32)
    o_ref[...] = (a * jax.nn.sigmoid(a) * b).astype(o_ref.dtype)


def _ffn_up(h, w_gate, w_up, layer, tm=1024, tn=512):
    t, k = h.shape
    n = w_gate.shape[2]
    wspec = pl.BlockSpec((None, k, tn), lambda j, i: (layer, 0, j))
    return pl.pallas_call(
        _ffn_up_kernel,
        grid=(n // tn, t // tm),
        in_specs=[pl.BlockSpec((tm, k), lambda j, i: (i, 0)), wspec, wspec],
        out_specs=pl.BlockSpec((tm, tn), lambda j, i: (i, j)),
        out_shape=jax.ShapeDtypeStruct((t, n), BF16),
        scratch_shapes=[pltpu.VMEM((k, tn), BF16), pltpu.VMEM((k, tn), BF16)],
        compiler_params=_cparams("parallel", "arbitrary"),
        name="ffn_up",
    )(h, w_gate, w_up)


def _mm_kernel(a_ref, b_ref, o_ref):
    o_ref[...] = jnp.dot(a_ref[...], b_ref[...], preferred_element_type=F32).astype(o_ref.dtype)


def _mm(a, b, out_dtype, tm=1024, tn=512):
    t, k = a.shape
    n = b.shape[1]
    return pl.pallas_call(
        _mm_kernel,
        grid=(t // tm, n // tn),
        in_specs=[pl.BlockSpec((tm, k), lambda i, j: (i, 0)), pl.BlockSpec((k, tn), lambda i, j: (0, j))],
        out_specs=pl.BlockSpec((tm, tn), lambda i, j: (i, j)),
        out_shape=jax.ShapeDtypeStruct((t, n), out_dtype),
        compiler_params=_cparams("parallel", "parallel"),
        name="proj_in",
    )(a, b)


def _mm_res_kernel(a_ref, b_ref, r_ref, o_ref, *, scale):
    o_ref[...] = r_ref[...] + scale * jnp.dot(a_ref[...], b_ref[...], preferred_element_type=F32)


def _mm_res(a, b, res, scale, tm=1024, tn=512):
    t, kdim = a.shape
    n = b.shape[1]
    return pl.pallas_call(
        functools.partial(_mm_res_kernel, scale=scale),
        grid=(t // tm, n // tn),
        in_specs=[pl.BlockSpec((tm, kdim), lambda i, j: (i, 0)),
                  pl.BlockSpec((kdim, tn), lambda i, j: (0, j)),
                  pl.BlockSpec((tm, tn), lambda i, j: (i, j))],
        out_specs=pl.BlockSpec((tm, tn), lambda i, j: (i, j)),
        out_shape=jax.ShapeDtypeStruct((t, n), F32),
        compiler_params=_cparams("parallel", "parallel"),
        name="mm_residual",
    )(a, b, res)


def _rope(x, cos2, sin2):
    return x * cos2 + pltpu.roll(x, HEAD_DIM // 2, axis=1) * sin2


def _nsa_prep_kernel(kc_ref, vc_ref, ks_ref, vs_ref, kw_ref, vw_ref, cos_ref, sin_ref, cv_ref, kv_ref):
    cos2 = cos_ref[...]
    sin2 = sin_ref[...]
    for g in range(NSA_KV_HEADS):
        sl = slice(g * HEAD_DIM, (g + 1) * HEAD_DIM)
        cv_ref[g] = _rope(kc_ref[:, sl], cos2, sin2)
        cv_ref[NSA_KV_HEADS + g] = vc_ref[:, sl]
        for j, (ref, roped) in enumerate(((ks_ref, True), (vs_ref, False), (kw_ref, True), (vw_ref, False))):
            x = ref[:, sl]
            if roped:
                x = _rope(x, cos2, sin2)
            c0 = (j * NSA_KV_HEADS + g) * HEAD_DIM
            kv_ref[:, c0:c0 + HEAD_DIM] = x.astype(BF16)


def _nsa_prep(p, cos2, sin2, seq, tile=512):
    t = p.shape[0]
    per_seq = seq // tile
    kvw = KV_WIDTH

    def pspec(cb):
        return pl.BlockSpec((tile, kvw), lambda i, cb=cb: (i, cb * LANE // kvw))

    tab = pl.BlockSpec((tile, HEAD_DIM), lambda i: (i % per_seq, 0))
    return pl.pallas_call(
        _nsa_prep_kernel,
        grid=(t // tile,),
        in_specs=[pspec(CB_NKC), pspec(CB_NVC), pspec(CB_NKS), pspec(CB_NVS), pspec(CB_NKW), pspec(CB_NVW), tab, tab],
        out_specs=[pl.BlockSpec((2 * NSA_KV_HEADS, tile, HEAD_DIM), lambda i: (0, i, 0)),
                   pl.BlockSpec((tile, 4 * kvw), lambda i: (i, 0))],
        out_shape=[jax.ShapeDtypeStruct((2 * NSA_KV_HEADS, t, HEAD_DIM), F32),
                   jax.ShapeDtypeStruct((t, 4 * kvw), BF16)],
        compiler_params=_cparams("parallel"),
        name="nsa_prep",
    )(p, p, p, p, p, p, cos2, sin2)


def _nsa_compress_kernel(t_ref, pe_ref, w1_ref, w2_ref, o_ref):
    x = t_ref[...]
    half = CMP_STRIDE * HEAD_DIM
    nr = x.shape[0]
    u = _dot(x + pe_ref[0:1, :], w1_ref[0:half, :], precision=HIGHEST)
    v = _dot(x + pe_ref[1:2, :], w1_ref[half:2 * half, :], precision=HIGHEST)
    pre = u + pltpu.roll(v, nr - 1, axis=0)
    o_ref[...] = _dot(jax.nn.gelu(pre), w2_ref[...], precision=HIGHEST)


def _nsa_compress(cv, pe, w1, w2, batch, seq):
    nr = seq // CMP_STRIDE
    half = CMP_STRIDE * HEAD_DIM
    t2 = cv.reshape(2 * NSA_KV_HEADS, batch, nr, half)
    return pl.pallas_call(
        _nsa_compress_kernel,
        grid=(2 * NSA_KV_HEADS, batch),
        in_specs=[pl.BlockSpec((None, None, nr, half), lambda c, b: (c, b, 0, 0)),
                  pl.BlockSpec((None, 2, half), lambda c, b: (c // NSA_KV_HEADS, 0, 0)),
                  pl.BlockSpec((None, 2 * half, HEAD_DIM), lambda c, b: (c // NSA_KV_HEADS, 0, 0)),
                  pl.BlockSpec((None, HEAD_DIM, HEAD_DIM), lambda c, b: (c // NSA_KV_HEADS, 0, 0))],
        out_specs=pl.BlockSpec((None, None, nr, HEAD_DIM), lambda c, b: (c, b, 0, 0)),
        out_shape=jax.ShapeDtypeStruct((2 * NSA_KV_HEADS, batch, nr, HEAD_DIM), F32),
        compiler_params=_cparams("parallel", "parallel"),
        name="nsa_compress",
    )(t2, pe.reshape(2, 2, half), w1, w2)


SEL_TILE = 512


def _masked_softmax(s, mask):
    s = jnp.where(mask, s, MASK_VALUE)
    m = jnp.max(s, axis=-1, keepdims=True)
    e = jnp.where(mask, jnp.exp(s - m), 0.0)
    return e / jnp.maximum(jnp.sum(e, axis=-1, keepdims=True), 1e-30)


def _nsa_attn_kernel(q_ref, gt_ref, cos_ref, sin_ref, kc_ref, vc_ref, ks_ref, vs_ref, kw_ref, vw_ref,
                     o_ref, m_ref, l_ref, acc_ref, *, seq):
    g = pl.program_id(1)
    i = pl.program_id(2)
    qn, r_n, d = Q_BLOCK, NSA_REP, HEAD_DIM
    rows = r_n * qn
    n_blk = seq // SEL_BLOCK
    n_sel = min(N_SEL, n_blk)
    ncr = seq // CMP_STRIDE

    cos2 = cos_ref[...]
    sin2 = sin_ref[...]
    qb = q_ref[...]
    q = jnp.concatenate([_rope(qb[:, r * d:(r + 1) * d], cos2, sin2) * (HEAD_DIM ** -0.5) for r in range(r_n)],
                        axis=0)
    q_bf = q.astype(BF16)
    t0 = i * qn
    t_col = t0 + lax.broadcasted_iota(jnp.int32, (qn, 1), 0)
    t_rows = jnp.concatenate([t_col] * r_n, axis=0)

    kc = kc_ref[...]
    vc = vc_ref[...]
    s_c = _dot3(_split_bf16(q), _split_bf16(kc), ((1,), (1,)))
    cmp_end = lax.broadcasted_iota(jnp.int32, (1, ncr), 1) * CMP_STRIDE + (CMP_BLOCK - 1)
    p_c = _masked_softmax(s_c, cmp_end <= t_rows)
    o_c = _dot(p_c.astype(BF16), vc.astype(BF16))

    pc_sum = p_c[0:qn]
    for r in range(1, r_n):
        pc_sum = pc_sum + p_c[r * qn:(r + 1) * qn]
    m_i = lax.broadcasted_iota(jnp.int32, (n_blk, ncr), 0)
    c_i = lax.broadcasted_iota(jnp.int32, (n_blk, ncr), 1)
    ov = (jnp.minimum(c_i * CMP_STRIDE + CMP_BLOCK, m_i * SEL_BLOCK + SEL_BLOCK)
          - jnp.maximum(c_i * CMP_STRIDE, m_i * SEL_BLOCK))
    ov_t = jnp.where(c_i < ncr - 1, jnp.maximum(ov, 0).astype(F32) / CMP_BLOCK, 0.0)
    ov_bf = ov_t.astype(BF16)
    pc_hi, pc_lo = _split_bf16(pc_sum)
    imp = _dot_nt(ov_bf, pc_hi) + _dot_nt(ov_bf, pc_lo)
    blk = lax.broadcasted_iota(jnp.int32, (n_blk, qn), 0)
    t_row = t0 + lax.broadcasted_iota(jnp.int32, (n_blk, qn), 1)
    cur = t_row // SEL_BLOCK
    forced = (blk == 0) | (blk == cur) | (blk == cur - 1)
    future = blk * SEL_BLOCK > t_row
    imp = jnp.where(forced, FORCE_VALUE, jnp.where(future, -FORCE_VALUE, imp))
    rank = jnp.zeros((n_blk, qn), F32)
    for j in range(n_blk):
        row = imp[j:j + 1, :]
        beats = (row > imp) | ((row == imp) & (blk > j))
        rank = rank + jnp.where(beats, 1.0, 0.0)
    sel_t = jnp.where(rank < n_sel, 1.0, 0.0)
    sel = sel_t.T.astype(BF16)

    m_ref[...] = jnp.full((rows, 1), MASK_VALUE, F32)
    l_ref[...] = jnp.zeros((rows, 1), F32)
    acc_ref[...] = jnp.zeros((rows, d), F32)
    n_tiles = (t0 + qn + SEL_TILE - 1) // SEL_TILE

    def sel_step(kt, carry):
        k0 = pl.multiple_of(kt * SEL_TILE, SEL_TILE)
        k = ks_ref[pl.ds(k0, SEL_TILE), :]
        v = vs_ref[pl.ds(k0, SEL_TILE), :]
        s = _dot_nt(q_bf, k)
        kpos = k0 + lax.broadcasted_iota(jnp.int32, (1, SEL_TILE), 1)
        e_blk = lax.broadcasted_iota(jnp.int32, (n_blk, SEL_TILE), 0)
        e_key = k0 + lax.broadcasted_iota(jnp.int32, (n_blk, SEL_TILE), 1)
        expand = jnp.where(e_blk == e_key // SEL_BLOCK, 1.0, 0.0).astype(BF16)
        chosen = _dot(sel, expand)
        ok_q = (chosen > 0.5) & (kpos <= t_col)
        ok = jnp.concatenate([ok_q] * r_n, axis=0)
        s = jnp.where(ok, s, MASK_VALUE)
        m_old = m_ref[...]
        m_new = jnp.maximum(m_old, jnp.max(s, axis=-1, keepdims=True))
        alpha = jnp.exp(m_old - m_new)
        p = jnp.exp(s - m_new)
        l_ref[...] = alpha * l_ref[...] + jnp.sum(p, axis=-1, keepdims=True)
        acc_ref[...] = alpha * acc_ref[...] + _dot(p.astype(BF16), v)
        m_ref[...] = m_new
        return carry

    lax.fori_loop(0, n_tiles, sel_step, 0)
    o_s = acc_ref[...] / jnp.maximum(l_ref[...], 1e-30)

    wlen = WINDOW + qn
    w0 = pl.multiple_of(jnp.maximum(t0 - WINDOW, 0), qn)
    kw = kw_ref[pl.ds(w0, wlen), :]
    vw = vw_ref[pl.ds(w0, wlen), :]
    s_w = _dot_nt(q_bf, kw)
    wpos = w0 + lax.broadcasted_iota(jnp.int32, (1, wlen), 1)
    p_w = _masked_softmax(s_w, (wpos <= t_rows) & (wpos > t_rows - WINDOW))
    o_w = _dot(p_w.astype(BF16), vw)

    gates = jax.nn.sigmoid(gt_ref[...])
    outs = []
    for r in range(r_n):
        sl = slice(r * qn, (r + 1) * qn)
        o_r = 0.0
        for c, o_b in enumerate((o_c, o_s, o_w)):
            lane = r * 3 + c
            gcol = jnp.where(g == 0, gates[:, lane:lane + 1], gates[:, 3 * r_n + lane:3 * r_n + lane + 1])
            o_r = o_r + gcol * o_b[sl]
        outs.append(o_r)
    o_ref[...] = jnp.concatenate(outs, axis=1).astype(o_ref.dtype)


def _nsa_attn(p, cmp_kv, kvbf, cos2, sin2, batch, seq):
    t = p.shape[0]
    n_qb = seq // Q_BLOCK
    ncr = seq // CMP_STRIDE
    gw = NSA_REP * HEAD_DIM
    rows = NSA_REP * Q_BLOCK

    def kv_spec(j):
        return pl.BlockSpec((seq, HEAD_DIM), lambda b, g, i, j=j: (b, j * NSA_KV_HEADS + g))

    return pl.pallas_call(
        functools.partial(_nsa_attn_kernel, seq=seq),
        grid=(batch, NSA_KV_HEADS, n_qb),
        in_specs=[pl.BlockSpec((Q_BLOCK, gw), lambda b, g, i: (b * n_qb + i, CB_NQ // NSA_REP + g)),
                  pl.BlockSpec((Q_BLOCK, LANE), lambda b, g, i: (b * n_qb + i, CB_NGT)),
                  pl.BlockSpec((Q_BLOCK, HEAD_DIM), lambda b, g, i: (i, 0)),
                  pl.BlockSpec((Q_BLOCK, HEAD_DIM), lambda b, g, i: (i, 0)),
                  pl.BlockSpec((None, None, ncr, HEAD_DIM), lambda b, g, i: (g, b, 0, 0)),
                  pl.BlockSpec((None, None, ncr, HEAD_DIM), lambda b, g, i: (NSA_KV_HEADS + g, b, 0, 0)),
                  kv_spec(0), kv_spec(1), kv_spec(2), kv_spec(3)],
        out_specs=pl.BlockSpec((Q_BLOCK, gw), lambda b, g, i: (b * n_qb + i, g)),
        out_shape=jax.ShapeDtypeStruct((t, NSA_WIDTH), BF16),
        scratch_shapes=[pltpu.VMEM((rows, 1), F32), pltpu.VMEM((rows, 1), F32), pltpu.VMEM((rows, HEAD_DIM), F32)],
        compiler_params=_cparams("parallel", "parallel", "arbitrary"),
        name="nsa_attn",
    )(p, p, cos2, sin2, cmp_kv, cmp_kv, kvbf, kvbf, kvbf, kvbf)


def _tri(n, lower_inclusive=True):
    r = lax.broadcasted_iota(jnp.int32, (n, n), 0)
    c = lax.broadcasted_iota(jnp.int32, (n, n), 1)
    return r, c


def _softplus(x):
    return jnp.maximum(x, 0.0) + jnp.log1p(jnp.exp(-jnp.abs(x)))


def _gated_rmsnorm(o, norm_w, gate):
    y = o * lax.rsqrt(jnp.mean(o * o, axis=-1, keepdims=True) + EPS)
    return (y * norm_w) * (gate * jax.nn.sigmoid(gate))


GDN_TILE = 256
GDN_SCAN_TILE = 128


def _split_bf16(x):
    hi = x.astype(BF16)
    return hi, (x - hi.astype(F32)).astype(BF16)


def _dot3(a, b, dims=((1,), (0,))):
    return _dot(a[0], b[0], dims) + (_dot(a[0], b[1], dims) + _dot(a[1], b[0], dims))


def _chunk_cumsum(x, chunk):
    pos = lax.broadcasted_iota(jnp.int32, (x.shape[0], 1), 0) % chunk
    s = 1
    while s < chunk:
        x = x + jnp.where(pos >= s, pltpu.roll(x, s, axis=0), 0.0)
        s *= 2
    return x


def _gdn_prep_kernel(q_ref, k_ref, v_ref, ab_ref, wconv_ref, alog_ref, dtb_ref,
                     u_ref, w_ref, qd_ref, att_ref, kdt_ref, dl_ref, tail_ref):
    c_n, d, h_n, tb = GDN_CHUNK, HEAD_DIM, GDN_HEADS, GDN_TILE
    nch = tb // c_n
    w = h_n * d

    @pl.when(pl.program_id(1) == 0)
    def _():
        tail_ref[...] = jnp.zeros_like(tail_ref)

    x = jnp.concatenate([q_ref[...], k_ref[...], v_ref[...]], axis=1)
    xcat = jnp.concatenate([tail_ref[...], x], axis=0)
    wc = wconv_ref[...]
    conv = None
    for j in range(CONV_WIDTH):
        shift = CONV_WIDTH - 1 - j
        xs = xcat if shift == 0 else pltpu.roll(xcat, shift, axis=0)
        term = xs[8:8 + tb] * wc[j:j + 1, :]
        conv = term if conv is None else conv + term
    tail_ref[...] = x[tb - 8:tb]
    act = conv * jax.nn.sigmoid(conv)

    ab = ab_ref[...]
    g_all = -jnp.exp(alog_ref[...]) * _softplus(ab + dtb_ref[...])
    beta_all = jax.nn.sigmoid(ab)
    gc_all = _chunk_cumsum(g_all, c_n)
    gl_all = jnp.concatenate([jnp.broadcast_to(gc_all[(c + 1) * c_n - 1:(c + 1) * c_n], (c_n, LANE))
                              for c in range(nch)], axis=0)
    gct_all = gc_all.T
    dl_ref[...] = jnp.exp(gl_all)
    r_i, c_i = _tri(tb)
    same = (r_i // c_n) == (c_i // c_n)
    causal = same & (r_i >= c_i)
    strict = same & (r_i > c_i)
    eye = jnp.where(r_i == c_i, 1.0, 0.0)
    lane_i = lax.broadcasted_iota(jnp.int32, (d, LANE), 1)

    for h in range(h_n):
        sl = slice(h * d, (h + 1) * d)
        xq = act[:, h * d:(h + 1) * d]
        xk = act[:, w + h * d:w + (h + 1) * d]
        v = act[:, 2 * w + h * d:2 * w + (h + 1) * d]
        q = xq * lax.rsqrt(jnp.sum(xq * xq, axis=-1, keepdims=True) + EPS) * (HEAD_DIM ** -0.5)
        k = xk * lax.rsqrt(jnp.sum(xk * xk, axis=-1, keepdims=True) + EPS)
        beta = beta_all[:, h_n + h:h_n + h + 1]
        gc = gc_all[:, h:h + 1]
        gc_row = gct_all[h:h + 1, :]
        decay = jnp.where(causal, jnp.exp(jnp.where(causal, gc - gc_row, 0.0)), 0.0)
        kb = k * beta
        qk = _dot_nt(jnp.concatenate([q, kb], axis=0).astype(BF16), k.astype(BF16))
        attn = qk[:tb] * decay
        a_mat = jnp.where(strict, qk[tb:] * decay, 0.0)
        egc = jnp.exp(gc)
        rhs = jnp.concatenate([v * beta, kb * egc], axis=1)
        pw = -a_mat
        t_inv = eye + pw
        pw_s = _split_bf16(pw)
        span = 2
        while span < c_n:
            pw = _dot3(pw_s, pw_s)
            pw_s = _split_bf16(pw)
            t_inv = t_inv + _dot3(_split_bf16(t_inv), pw_s)
            span *= 2
        sol = _dot3(_split_bf16(t_inv), _split_bf16(rhs))
        u_ref[:, sl] = sol[:, :d]
        w_ref[:, sl] = sol[:, d:].astype(BF16)
        qd_ref[:, sl] = (q * egc).astype(BF16)
        att_ref[:, sl] = jnp.concatenate([attn[j * LANE:(j + 1) * LANE, j * LANE:(j + 1) * LANE]
                                          for j in range(tb // LANE)], axis=0).astype(BF16)
        kdt = (k * jnp.exp(gl_all[:, h:h + 1] - gc)).T
        for c in range(nch):
            blk = kdt[:, (c // 2) * LANE:(c // 2 + 1) * LANE]
            if c % 2:
                blk = pltpu.roll(blk, c_n, axis=1)
            kdt_ref[c, h] = jnp.where(lane_i < c_n, blk, 0.0).astype(BF16)


def _gdn_scan_kernel(u_ref, w_ref, qd_ref, att_ref, kdt_ref, dl_ref, z_ref, nw_ref, o_ref, state_ref):
    c_n, d, h_n = GDN_CHUNK, HEAD_DIM, GDN_HEADS

    @pl.when(pl.program_id(1) == 0)
    def _():
        state_ref[...] = jnp.zeros_like(state_ref)

    nw = nw_ref[...]
    for c in range(GDN_SCAN_TILE // c_n):
        rs = slice(c * c_n, (c + 1) * c_n)
        for h in range(h_n):
            sl = slice(h * d, (h + 1) * d)
            state = state_ref[h]
            r = _dot(jnp.concatenate([w_ref[rs, sl], qd_ref[rs, sl]], axis=0), state.astype(BF16))
            v_new = (u_ref[rs, sl] - r[:c_n]).astype(BF16)
            vv = jnp.concatenate([v_new, v_new], axis=0)
            o = r[c_n:] + _dot(att_ref[rs, sl], vv)
            state_ref[h] = state * dl_ref[c * c_n:c * c_n + 1, h:h + 1] + _dot(kdt_ref[c, h], vv)
            o_ref[rs, sl] = _gated_rmsnorm(o, nw, z_ref[rs, sl]).astype(o_ref.dtype)


def _gdn(p, w_conv, a_log, dt_bias, norm_w, batch, seq):
    t = p.shape[0]
    c_n, h_n, d = GDN_CHUNK, GDN_HEADS, HEAD_DIM
    w = h_n * d
    tb, ts = GDN_TILE, GDN_SCAN_TILE
    pad = jnp.zeros((LANE - h_n,), F32)
    alog = jnp.concatenate([a_log, pad]).reshape(1, LANE)
    dtb = jnp.concatenate([dt_bias, pad]).reshape(1, LANE)

    def whole(shape):
        return pl.BlockSpec(shape, lambda b, c: (0,) * len(shape))

    n_t = seq // tb

    def pspec(cb, width):
        return pl.BlockSpec((tb, width), lambda b, i, cb=cb, width=width: (b * n_t + i, cb * LANE // width))

    row_w = pl.BlockSpec((tb, w), lambda b, i: (b * n_t + i, 0))
    u, wm, qd, att, kdt, dl = pl.pallas_call(
        _gdn_prep_kernel,
        grid=(batch, n_t),
        in_specs=[pspec(CB_GQ, w), pspec(CB_GK, w), pspec(CB_GV, w), pspec(CB_GAB, LANE),
                  whole((CONV_WIDTH, 3 * w)), whole((1, LANE)), whole((1, LANE))],
        out_specs=[row_w, row_w, row_w, row_w,
                   pl.BlockSpec((tb // c_n, h_n, d, LANE), lambda b, i: (b * n_t + i, 0, 0, 0)),
                   pl.BlockSpec((tb, LANE), lambda b, i: (b * n_t + i, 0))],
        out_shape=[jax.ShapeDtypeStruct((t, w), F32), jax.ShapeDtypeStruct((t, w), BF16),
                   jax.ShapeDtypeStruct((t, w), BF16), jax.ShapeDtypeStruct((t, w), BF16),
                   jax.ShapeDtypeStruct((t // c_n, h_n, d, LANE), BF16), jax.ShapeDtypeStruct((t, LANE), F32)],
        scratch_shapes=[pltpu.VMEM((8, 3 * w), F32)],
        compiler_params=_cparams("parallel", "arbitrary"),
        name="gdn_prep",
    )(p, p, p, p, w_conv, alog, dtb)

    n_s = seq // ts
    row_s = pl.BlockSpec((ts, w), lambda b, i: (b * n_s + i, 0))
    return pl.pallas_call(
        _gdn_scan_kernel,
        grid=(batch, n_s),
        in_specs=[row_s, row_s, row_s, row_s,
                  pl.BlockSpec((ts // c_n, h_n, d, LANE), lambda b, i: (b * n_s + i, 0, 0, 0)),
                  pl.BlockSpec((ts, LANE), lambda b, i: (b * n_s + i, 0)),
                  pl.BlockSpec((ts, w), lambda b, i: (b * n_s + i, CB_GZ * LANE // w)),
                  whole((1, d))],
        out_specs=row_s,
        out_shape=jax.ShapeDtypeStruct((t, w), BF16),
        scratch_shapes=[pltpu.VMEM((h_n, d, d), F32)],
        compiler_params=_cparams("parallel", "arbitrary"),
        name="gdn_scan",
    )(u, wm, qd, att, kdt, dl, p, norm_w.reshape(1, d))


HGRN_SUB = 16


def _hgrn_kernel(q_ref, f_ref, i_ref, g_ref, lb_ref, nw_ref, o_ref, state_ref):
    c_n, d, h_n, sb = HGRN_CHUNK, HEAD_DIM, HGRN_HEADS, HGRN_SUB
    ci = pl.program_id(1)

    @pl.when(ci == 0)
    def _():
        state_ref[...] = jnp.zeros_like(state_ref)

    lane_i = lax.broadcasted_iota(jnp.int32, (sb, c_n), 1)
    nw = nw_ref[...]
    outs = []
    for h in range(h_n):
        sl = slice(h * d, (h + 1) * d)
        lb = lb_ref[:, sl]
        f_gate = lb + (1.0 - lb) * jax.nn.sigmoid(f_ref[:, sl])
        log_f = jnp.log(jnp.maximum(f_gate, MIN_FORGET))
        k = 1.0 - f_gate
        q = q_ref[:, sl]
        v = i_ref[:, sl]
        bcum = _chunk_cumsum(log_f, c_n)
        a_rows = []
        for sbi in range(c_n // sb):
            r0 = sbi * sb
            q_s = q[r0:r0 + sb]
            b_s = bcum[r0:r0 + sb]
            a_blk = jnp.zeros((sb, c_n), F32)
            if sbi > 0:
                b_ref0 = bcum[r0:r0 + 1]
                q_t = q_s * jnp.exp(b_s - b_ref0)
                k_t = k * jnp.exp(jnp.minimum(b_ref0 - bcum, 0.0))
                a_blk = jnp.where(lane_i < r0, _dot3(_split_bf16(q_t), _split_bf16(k_t), ((1,), (1,))), 0.0)
            for jj in range(sb):
                j = r0 + jj
                ok = lax.broadcasted_iota(jnp.int32, (sb, 1), 0) >= jj
                e = jnp.where(ok, jnp.exp(jnp.where(ok, b_s - bcum[j:j + 1], 0.0)), 0.0)
                col = jnp.sum(q_s * k[j:j + 1] * e, axis=-1, keepdims=True)
                a_blk = jnp.where(lane_i == j, col, a_blk)
            a_rows.append(a_blk)
        a_mat = jnp.concatenate(a_rows, axis=0)
        state_t = state_ref[h]
        v_bf = v.astype(BF16)
        o = _dot_nt((q * jnp.exp(bcum)).astype(BF16), state_t.astype(BF16)) + _dot(a_mat.astype(BF16), v_bf)
        b_last = bcum[c_n - 1:c_n]
        k_dec = k * jnp.exp(b_last - bcum)
        state_ref[h] = state_t * jnp.exp(b_last) + _dot(v.T.astype(BF16), k_dec.astype(BF16))
        outs.append(_gated_rmsnorm(o, nw, g_ref[:, sl]))
    o_ref[...] = jnp.concatenate(outs, axis=1).astype(o_ref.dtype)


def _hgrn(p, lb, norm_w, batch, seq):
    t = p.shape[0]
    c_n, h_n, d = HGRN_CHUNK, HGRN_HEADS, HEAD_DIM
    w = h_n * d
    n_c = seq // c_n

    def pspec(cb):
        return pl.BlockSpec((c_n, w), lambda b, c, cb=cb: (b * n_c + c, cb * LANE // w))

    return pl.pallas_call(
        _hgrn_kernel,
        grid=(batch, n_c),
        in_specs=[pspec(CB_HQ), pspec(CB_HF), pspec(CB_HI), pspec(CB_HG),
                  pl.BlockSpec((1, w), lambda b, c: (0, 0)), pl.BlockSpec((1, d), lambda b, c: (0, 0))],
        out_specs=pl.BlockSpec((c_n, w), lambda b, c: (b * n_c + c, 0)),
        out_shape=jax.ShapeDtypeStruct((t, w), BF16),
        scratch_shapes=[pltpu.VMEM((h_n, d, d), F32)],
        compiler_params=_cparams("parallel", "arbitrary"),
        name="hgrn",
    )(p, p, p, p, lb.reshape(1, w), norm_w.reshape(1, d))


def _round_up(x, m):
    return -(-x // m) * m


def _pack_plan():
    offs = [int(v) for v in np.concatenate([[0], np.cumsum(np.array(IN_SIZES))])]
    plan = []
    dst = 0
    for j in (8, 9, 10, 13, 14, 15, 16, 17, 1, 2, 3, 4, 5, 6):
        plan.append((offs[j], IN_SIZES[j], dst))
        dst += IN_SIZES[j]
    plan.append((offs[7], IN_SIZES[7], CB_NGT * LANE))
    plan.append((offs[11], IN_SIZES[11] + IN_SIZES[12], CB_GAB * LANE))
    plan.append((offs[0], IN_SIZES[0], CB_NQ * LANE))
    return tuple(plan)


def _pack_kernel(w_ref, o_ref):
    for src, width, dst in _pack_plan():
        a0 = src // LANE * LANE
        off = src - a0
        span = _round_up(off + width, LANE)
        x = w_ref[:, a0:a0 + span]
        if off:
            x = pltpu.roll(x, span - off, axis=1)
        wout = _round_up(width, LANE)
        x = x[:, :wout]
        if width % LANE:
            x = jnp.where(lax.broadcasted_iota(jnp.int32, x.shape, 1) < width, x, 0.0)
        o_ref[:, dst:dst + wout] = x.astype(o_ref.dtype)


def _pack_w_in(w_in, layer, tr=256):
    _, k, n = w_in.shape
    return pl.pallas_call(
        _pack_kernel,
        grid=(k // tr,),
        in_specs=[pl.BlockSpec((None, tr, _round_up(n, LANE)), lambda i: (layer, i, 0))],
        out_specs=pl.BlockSpec((tr, P_WIDTH), lambda i: (i, 0)),
        out_shape=jax.ShapeDtypeStruct((k, P_WIDTH), BF16),
        compiler_params=_cparams("parallel"),
        name="pack_w_in",
    )(w_in)


def _rope_tables(seq):
    inv = 1.0 / (ROPE_THETA ** (jnp.arange(0, HEAD_DIM, 2, dtype=F32) / HEAD_DIM))
    ang = jnp.arange(seq, dtype=F32)[:, None] * inv[None, :]
    cos, sin = jnp.cos(ang), jnp.sin(ang)
    return jnp.concatenate([cos, cos], axis=1), jnp.concatenate([-sin, sin], axis=1)


def _mixers(p, batch, seq, layer_params):
    (pe_k, pe_v, ck1, ck2, cv1, cv2, conv_w, a_log, dt_bias, gdn_norm, lb, hgrn_norm, cos2, sin2) = layer_params
    cv, kvbf = _nsa_prep(p, cos2, sin2, seq)
    cmp_kv = _nsa_compress(cv, jnp.stack([pe_k, pe_v]), jnp.stack([ck1, cv1]), jnp.stack([ck2, cv2]), batch, seq)
    y_nsa = _nsa_attn(p, cmp_kv, kvbf, cos2, sin2, batch, seq)
    y_gdn = _gdn(p, conv_w, a_log, dt_bias, gdn_norm, batch, seq)
    y_hgrn = _hgrn(p, lb, hgrn_norm, batch, seq)
    return jnp.concatenate([y_nsa, y_gdn, y_hgrn], axis=1)


def kernel(x, ffn1_norm, ffn1_gate, ffn1_up, ffn1_down, mix_norm, w_in, w_out, nsa_pe_k, nsa_pe_v, nsa_ck1, nsa_ck2, nsa_cv1, nsa_cv2, gdn_conv, gdn_a_log, gdn_dt_bias, gdn_norm, hgrn_lb, hgrn_norm, ffn2_norm, ffn2_gate, ffn2_up, ffn2_down, final_norm):
    batch, seq, dm = x.shape
    depth = w_in.shape[0]
    cos2, sin2 = _rope_tables(seq)
    p_lb = jax.nn.softmax(hgrn_lb.astype(F32), axis=0)
    lb_all = jnp.cumsum(p_lb, axis=0) - p_lb[0:1]
    xt = x.reshape(batch * seq, dm)

    def ffn(xt, norm, wg, wu, wd, l):
        h = _rmsnorm(xt, norm[l], BF16)
        act = _ffn_up(h, wg, wu, l)
        return _mm_res(act, wd[l].astype(BF16), xt, 0.5)

    for l in range(depth):
        xt = ffn(xt, ffn1_norm, ffn1_gate, ffn1_up, ffn1_down, l)
        h = _rmsnorm(xt, mix_norm[l], BF16)
        p = _mm(h, _pack_w_in(w_in, l), F32)
        y = _mixers(p, batch, seq, (nsa_pe_k[l], nsa_pe_v[l], nsa_ck1[l], nsa_ck2[l], nsa_cv1[l], nsa_cv2[l],
                                    gdn_conv[l], gdn_a_log[l], gdn_dt_bias[l], gdn_norm[l], lb_all[l], hgrn_norm[l],
                                    cos2, sin2))
        xt = _mm_res(y, w_out[l].astype(BF16), xt, 1.0)
        xt = ffn(xt, ffn2_norm, ffn2_gate, ffn2_up, ffn2_down, l)
    return _rmsnorm(xt, final_norm, F32).reshape(batch, seq, dm)
```

```python
import functools

import jax
import jax.numpy as jnp
import numpy as np
from jax import lax
from jax.experimental import pallas as pl
from jax.experimental.pallas import tpu as pltpu

F32 = jnp.float32
BF16 = jnp.bfloat16
HIGHEST = lax.Precision.HIGHEST

D_MODEL = 2048
HEAD_DIM = 128
ROPE_THETA = 10000.0
EPS = 1e-6
MASK_VALUE = -1e30
FORCE_VALUE = 1e9
MIN_FORGET = 1e-6
NSA_HEADS = 6
NSA_KV_HEADS = 2
NSA_REP = NSA_HEADS // NSA_KV_HEADS
CMP_BLOCK = 32
CMP_STRIDE = 16
SEL_BLOCK = 64
N_SEL = 16
WINDOW = 512
Q_BLOCK = 128
GDN_HEADS = 5
CONV_WIDTH = 4
GDN_CHUNK = 64
HGRN_HEADS = 5
HGRN_CHUNK = 64
D_FF = 5632

NSA_WIDTH = NSA_HEADS * HEAD_DIM
KV_WIDTH = NSA_KV_HEADS * HEAD_DIM
GDN_WIDTH = GDN_HEADS * HEAD_DIM
HGRN_WIDTH = HGRN_HEADS * HEAD_DIM
MIX_WIDTH = NSA_WIDTH + GDN_WIDTH + HGRN_WIDTH
IN_SIZES = (NSA_WIDTH, KV_WIDTH, KV_WIDTH, KV_WIDTH, KV_WIDTH, KV_WIDTH, KV_WIDTH, 3 * NSA_HEADS,
            GDN_WIDTH, GDN_WIDTH, GDN_WIDTH, GDN_HEADS, GDN_HEADS, GDN_WIDTH,
            HGRN_WIDTH, HGRN_WIDTH, HGRN_WIDTH, HGRN_WIDTH)

LANE = 128
VMEM_LIMIT = 48 * 1024 * 1024

CB_GQ = 0
CB_GK = 5
CB_GV = 10
CB_GZ = 15
CB_HQ = 20
CB_HF = 25
CB_HI = 30
CB_HG = 35
CB_NKC = 40
CB_NVC = 42
CB_NKS = 44
CB_NVS = 46
CB_NKW = 48
CB_NVW = 50
CB_NGT = 52
CB_GAB = 53
CB_NQ = 54
P_BLOCKS = 60
P_WIDTH = P_BLOCKS * LANE


def _cparams(*sem):
    return pltpu.CompilerParams(dimension_semantics=sem, vmem_limit_bytes=VMEM_LIMIT)


def _dot(a, b, dims=((1,), (0,)), precision=None):
    return lax.dot_general(a, b, (dims, ((), ())), precision=precision, preferred_element_type=F32)


def _dot_nt(a, b, precision=None):
    return _dot(a, b, ((1,), (1,)), precision)


def _dot_tn(a, b, precision=None):
    return _dot(a, b, ((0,), (0,)), precision)


def _rmsnorm_kernel(x_ref, g_ref, o_ref):
    x = x_ref[...]
    y = x * lax.rsqrt(jnp.mean(x * x, axis=-1, keepdims=True) + EPS)
    o_ref[...] = (y * g_ref[...]).astype(o_ref.dtype)


def _rmsnorm(x, g, out_dtype, tm=512):
    t, d = x.shape
    return pl.pallas_call(
        _rmsnorm_kernel,
        grid=(t // tm,),
        in_specs=[pl.BlockSpec((tm, d), lambda i: (i, 0)), pl.BlockSpec((1, d), lambda i: (0, 0))],
        out_specs=pl.BlockSpec((tm, d), lambda i: (i, 0)),
        out_shape=jax.ShapeDtypeStruct((t, d), out_dtype),
        compiler_params=_cparams("parallel"),
        name="rmsnorm",
    )(x, g.reshape(1, d))


def _ffn_up_kernel(h_ref, wg_ref, wu_ref, o_ref, wg_bf, wu_bf):
    @pl.when(pl.program_id(1) == 0)
    def _():
        wg_bf[...] = wg_ref[...].astype(BF16)
        wu_bf[...] = wu_ref[...].astype(BF16)

    h = h_ref[...]
    a = jnp.dot(h, wg_bf[...], preferred_element_type=F32)
    b = jnp.dot(h, wu_bf[...], preferred_element_type=F32)
    o_ref[...] = (a * jax.nn.sigmoid(a) * b).astype(o_ref.dtype)


def _ffn_up(h, w_gate, w_up, layer, tm=1024, tn=512):
    t, k = h.shape
    n = w_gate.shape[2]
    wspec = pl.BlockSpec((None, k, tn), lambda j, i: (layer, 0, j))
    return pl.pallas_call(
        _ffn_up_kernel,
        grid=(n // tn, t // tm),
        in_specs=[pl.BlockSpec((tm, k), lambda j, i: (i, 0)), wspec, wspec],
        out_specs=pl.BlockSpec((tm, tn), lambda j, i: (i, j)),
        out_shape=jax.ShapeDtypeStruct((t, n), BF16),
        scratch_shapes=[pltpu.VMEM((k, tn), BF16), pltpu.VMEM((k, tn), BF16)],
        compiler_params=_cparams("parallel", "arbitrary"),
        name="ffn_up",
    )(h, w_gate, w_up)


def _mm_kernel(a_ref, b_ref, o_ref):
    o_ref[...] = jnp.dot(a_ref[...], b_ref[...], preferred_element_type=F32).astype(o_ref.dtype)


def _mm(a, b, out_dtype, tm=1024, tn=512):
    t, k = a.shape
    n = b.shape[1]
    return pl.pallas_call(
        _mm_kernel,
        grid=(t // tm, n // tn),
        in_specs=[pl.BlockSpec((tm, k), lambda i, j: (i, 0)), pl.BlockSpec((k, tn), lambda i, j: (0, j))],
        out_specs=pl.BlockSpec((tm, tn), lambda i, j: (i, j)),
        out_shape=jax.ShapeDtypeStruct((t, n), out_dtype),
        compiler_params=_cparams("parallel", "parallel"),
        name="proj_in",
    )(a, b)


def _mm_res_kernel(a_ref, b_ref, r_ref, o_ref, *, scale):
    o_ref[...] = r_ref[...] + scale * jnp.dot(a_ref[...], b_ref[...], preferred_element_type=F32)


def _mm_res(a, b, res, scale, tm=1024, tn=512):
    t, kdim = a.shape
    n = b.shape[1]
    return pl.pallas_call(
        functools.partial(_mm_res_kernel, scale=scale),
        grid=(t // tm, n // tn),
        in_specs=[pl.BlockSpec((tm, kdim), lambda i, j: (i, 0)),
                  pl.BlockSpec((kdim, tn), lambda i, j: (0, j)),
                  pl.BlockSpec((tm, tn), lambda i, j: (i, j))],
        out_specs=pl.BlockSpec((tm, tn), lambda i, j: (i, j)),
        out_shape=jax.ShapeDtypeStruct((t, n), F32),
        compiler_params=_cparams("parallel", "parallel"),
        name="mm_residual",
    )(a, b, res)


def _rope(x, cos2, sin2):
    return x * cos2 + pltpu.roll(x, HEAD_DIM // 2, axis=1) * sin2


def _nsa_prep_kernel(kc_ref, vc_ref, ks_ref, vs_ref, kw_ref, vw_ref, cos_ref, sin_ref, cv_ref, kv_ref, *, per_seq):
    cos2 = cos_ref[...]
    sin2 = sin_ref[...]
    tile = cos2.shape[0]
    pos = (pl.program_id(0) % per_seq) * tile + lax.broadcasted_iota(jnp.int32, (tile, LANE), 0)
    blk_hot = jnp.where(pos // SEL_BLOCK == lax.broadcasted_iota(jnp.int32, (tile, LANE), 1), 1.0, 0.0).astype(BF16)
    for g in range(NSA_KV_HEADS):
        sl = slice(g * HEAD_DIM, (g + 1) * HEAD_DIM)
        cv_ref[g] = _rope(kc_ref[:, sl], cos2, sin2)
        cv_ref[NSA_KV_HEADS + g] = vc_ref[:, sl]
        c0 = g * 2 * HEAD_DIM
        kv_ref[:, c0:c0 + HEAD_DIM] = _rope(ks_ref[:, sl], cos2, sin2).astype(BF16)
        kv_ref[:, c0 + HEAD_DIM:c0 + 2 * HEAD_DIM] = blk_hot
        for j, (ref, roped) in enumerate(((vs_ref, False), (kw_ref, True), (vw_ref, False))):
            x = ref[:, sl]
            if roped:
                x = _rope(x, cos2, sin2)
            c0 = ((j + 2) * NSA_KV_HEADS + g) * HEAD_DIM
            kv_ref[:, c0:c0 + HEAD_DIM] = x.astype(BF16)


def _nsa_prep(p, cos2, sin2, seq, tile=512):
    t = p.shape[0]
    per_seq = seq // tile
    kvw = KV_WIDTH
    assert seq // SEL_BLOCK <= LANE

    def pspec(cb):
        return pl.BlockSpec((tile, kvw), lambda i, cb=cb: (i, cb * LANE // kvw))

    tab = pl.BlockSpec((tile, HEAD_DIM), lambda i: (i % per_seq, 0))
    return pl.pallas_call(
        functools.partial(_nsa_prep_kernel, per_seq=per_seq),
        grid=(t // tile,),
        in_specs=[pspec(CB_NKC), pspec(CB_NVC), pspec(CB_NKS), pspec(CB_NVS), pspec(CB_NKW), pspec(CB_NVW), tab, tab],
        out_specs=[pl.BlockSpec((2 * NSA_KV_HEADS, tile, HEAD_DIM), lambda i: (0, i, 0)),
                   pl.BlockSpec((tile, 5 * kvw), lambda i: (i, 0))],
        out_shape=[jax.ShapeDtypeStruct((2 * NSA_KV_HEADS, t, HEAD_DIM), F32),
                   jax.ShapeDtypeStruct((t, 5 * kvw), BF16)],
        compiler_params=_cparams("parallel"),
        name="nsa_prep",
    )(p, p, p, p, p, p, cos2, sin2)


def _nsa_compress_kernel(t_ref, pe_ref, w1_ref, w2_ref, o_ref):
    x = t_ref[...]
    half = CMP_STRIDE * HEAD_DIM
    nr = x.shape[0]
    u = _dot(x + pe_ref[0:1, :], w1_ref[0:half, :], precision=HIGHEST)
    v = _dot(x + pe_ref[1:2, :], w1_ref[half:2 * half, :], precision=HIGHEST)
    pre = u + pltpu.roll(v, nr - 1, axis=0)
    o_ref[...] = _dot(jax.nn.gelu(pre), w2_ref[...], precision=HIGHEST)


def _nsa_compress(cv, pe, w1, w2, batch, seq):
    nr = seq // CMP_STRIDE
    half = CMP_STRIDE * HEAD_DIM
    t2 = cv.reshape(2 * NSA_KV_HEADS, batch, nr, half)
    return pl.pallas_call(
        _nsa_compress_kernel,
        grid=(2 * NSA_KV_HEADS, batch),
        in_specs=[pl.BlockSpec((None, None, nr, half), lambda c, b: (c, b, 0, 0)),
                  pl.BlockSpec((None, 2, half), lambda c, b: (c // NSA_KV_HEADS, 0, 0)),
                  pl.BlockSpec((None, 2 * half, HEAD_DIM), lambda c, b: (c // NSA_KV_HEADS, 0, 0)),
                  pl.BlockSpec((None, HEAD_DIM, HEAD_DIM), lambda c, b: (c // NSA_KV_HEADS, 0, 0))],
        out_specs=pl.BlockSpec((None, None, nr, HEAD_DIM), lambda c, b: (c, b, 0, 0)),
        out_shape=jax.ShapeDtypeStruct((2 * NSA_KV_HEADS, batch, nr, HEAD_DIM), F32),
        compiler_params=_cparams("parallel", "parallel"),
        name="nsa_compress",
    )(t2, pe.reshape(2, 2, half), w1, w2)


SEL_TILE = 512


def _masked_softmax(s, mask):
    s = jnp.where(mask, s, MASK_VALUE)
    m = jnp.max(s, axis=-1, keepdims=True)
    e = jnp.where(mask, jnp.exp(s - m), 0.0)
    return e / jnp.maximum(jnp.sum(e, axis=-1, keepdims=True), 1e-30)


def _nsa_attn_kernel(q_ref, gt_ref, cos_ref, sin_ref, kc_ref, vc_ref, ks_ref, vs_ref, kw_ref, vw_ref,
                     o_ref, m_ref, l_ref, acc_ref, *, seq):
    g = pl.program_id(1)
    i = pl.program_id(2)
    qn, r_n, d = Q_BLOCK, NSA_REP, HEAD_DIM
    rows = r_n * qn
    n_blk = seq // SEL_BLOCK
    n_sel = min(N_SEL, n_blk)
    ncr = seq // CMP_STRIDE

    cos2 = cos_ref[...]
    sin2 = sin_ref[...]
    qb = q_ref[...]
    q = jnp.concatenate([_rope(qb[:, r * d:(r + 1) * d], cos2, sin2) * (HEAD_DIM ** -0.5) for r in range(r_n)],
                        axis=0)
    q_bf = q.astype(BF16)
    t0 = i * qn
    t_col = t0 + lax.broadcasted_iota(jnp.int32, (qn, 1), 0)
    t_rows = jnp.concatenate([t_col] * r_n, axis=0)

    kc = kc_ref[...]
    vc = vc_ref[...]
    s_c = _dot3(_split_bf16(q), _split_bf16(kc), ((1,), (1,)))
    cmp_end = lax.broadcasted_iota(jnp.int32, (1, ncr), 1) * CMP_STRIDE + (CMP_BLOCK - 1)
    p_c = _masked_softmax(s_c, cmp_end <= t_rows)
    o_c = _dot(p_c.astype(BF16), vc.astype(BF16))

    pc_sum = p_c[0:qn]
    for r in range(1, r_n):
        pc_sum = pc_sum + p_c[r * qn:(r + 1) * qn]
    m_i = lax.broadcasted_iota(jnp.int32, (n_blk, ncr), 0)
    c_i = lax.broadcasted_iota(jnp.int32, (n_blk, ncr), 1)
    ov = (jnp.minimum(c_i * CMP_STRIDE + CMP_BLOCK, m_i * SEL_BLOCK + SEL_BLOCK)
          - jnp.maximum(c_i * CMP_STRIDE, m_i * SEL_BLOCK))
    ov_t = jnp.where(c_i < ncr - 1, jnp.maximum(ov, 0).astype(F32) / CMP_BLOCK, 0.0)
    ov_bf = ov_t.astype(BF16)
    pc_hi, pc_lo = _split_bf16(pc_sum)
    imp = _dot_nt(ov_bf, pc_hi) + _dot_nt(ov_bf, pc_lo)
    blk = lax.broadcasted_iota(jnp.int32, (n_blk, qn), 0)
    t_row = t0 + lax.broadcasted_iota(jnp.int32, (n_blk, qn), 1)
    cur = t_row // SEL_BLOCK
    forced = (blk == 0) | (blk == cur) | (blk == cur - 1)
    future = blk * SEL_BLOCK > t_row
    imp = jnp.where(forced, FORCE_VALUE, jnp.where(future, -FORCE_VALUE, imp))
    rank = jnp.zeros((n_blk, qn), F32)
    for j in range(n_blk):
        row = imp[j:j + 1, :]
        beats = (row > imp) | ((row == imp) & (blk > j))
        rank = rank + jnp.where(beats, 1.0, 0.0)
    bias_t = jnp.where(rank < n_sel, 0.0, MASK_VALUE)
    if n_blk < LANE:
        bias_t = jnp.concatenate([bias_t, jnp.zeros((LANE - n_blk, qn), F32)], axis=0)
    bias = bias_t.T.astype(BF16)
    q_aug = jnp.concatenate([q_bf, jnp.concatenate([bias] * r_n, axis=0)], axis=1)

    m_ref[...] = jnp.full((rows, 1), MASK_VALUE, F32)
    l_ref[...] = jnp.zeros((rows, 1), F32)
    acc_ref[...] = jnp.zeros((rows, d), F32)
    last_tile = (t0 + qn - 1) // SEL_TILE

    def sel_step(kt, causal):
        k0 = pl.multiple_of(kt * SEL_TILE, SEL_TILE)
        s = _dot_nt(q_aug, ks_ref[pl.ds(k0, SEL_TILE), :])
        if causal:
            kpos = k0 + lax.broadcasted_iota(jnp.int32, (1, SEL_TILE), 1)
            s = s + jnp.concatenate([jnp.where(kpos <= t_col, 0.0, MASK_VALUE)] * r_n, axis=0)
        m_old = m_ref[...]
        m_new = jnp.maximum(m_old, jnp.max(s, axis=-1, keepdims=True))
        alpha = jnp.exp(m_old - m_new)
        p = jnp.exp(s - m_new)
        l_ref[...] = alpha * l_ref[...] + jnp.sum(p, axis=-1, keepdims=True)
        acc_ref[...] = alpha * acc_ref[...] + _dot(p.astype(BF16), vs_ref[pl.ds(k0, SEL_TILE), :])
        m_ref[...] = m_new

    def sel_body(kt, carry):
        sel_step(kt, False)
        return carry

    lax.fori_loop(0, last_tile, sel_body, 0)
    sel_step(last_tile, True)
    o_s = acc_ref[...] / jnp.maximum(l_ref[...], 1e-30)

    wlen = WINDOW + qn
    w0 = pl.multiple_of(jnp.maximum(t0 - WINDOW, 0), qn)
    kw = kw_ref[pl.ds(w0, wlen), :]
    vw = vw_ref[pl.ds(w0, wlen), :]
    s_w = _dot_nt(q_bf, kw)
    wpos = w0 + lax.broadcasted_iota(jnp.int32, (1, wlen), 1)
    p_w = _masked_softmax(s_w, (wpos <= t_rows) & (wpos > t_rows - WINDOW))
    o_w = _dot(p_w.astype(BF16), vw)

    gates = jax.nn.sigmoid(gt_ref[...])
    outs = []
    for r in range(r_n):
        sl = slice(r * qn, (r + 1) * qn)
        o_r = 0.0
        for c, o_b in enumerate((o_c, o_s, o_w)):
            lane = r * 3 + c
            gcol = jnp.where(g == 0, gates[:, lane:lane + 1], gates[:, 3 * r_n + lane:3 * r_n + lane + 1])
            o_r = o_r + gcol * o_b[sl]
        outs.append(o_r)
    o_ref[...] = jnp.concatenate(outs, axis=1).astype(o_ref.dtype)


def _nsa_attn(p, cmp_kv, kvbf, cos2, sin2, batch, seq):
    t = p.shape[0]
    n_qb = seq // Q_BLOCK
    ncr = seq // CMP_STRIDE
    gw = NSA_REP * HEAD_DIM
    rows = NSA_REP * Q_BLOCK

    def kv_spec(j):
        if j == 0:
            return pl.BlockSpec((seq, 2 * HEAD_DIM), lambda b, g, i: (b, g))
        return pl.BlockSpec((seq, HEAD_DIM), lambda b, g, i, j=j: (b, (j + 1) * NSA_KV_HEADS + g))

    return pl.pallas_call(
        functools.partial(_nsa_attn_kernel, seq=seq),
        grid=(batch, NSA_KV_HEADS, n_qb),
        in_specs=[pl.BlockSpec((Q_BLOCK, gw), lambda b, g, i: (b * n_qb + i, CB_NQ // NSA_REP + g)),
                  pl.BlockSpec((Q_BLOCK, LANE), lambda b, g, i: (b * n_qb + i, CB_NGT)),
                  pl.BlockSpec((Q_BLOCK, HEAD_DIM), lambda b, g, i: (i, 0)),
                  pl.BlockSpec((Q_BLOCK, HEAD_DIM), lambda b, g, i: (i, 0)),
                  pl.BlockSpec((None, None, ncr, HEAD_DIM), lambda b, g, i: (g, b, 0, 0)),
                  pl.BlockSpec((None, None, ncr, HEAD_DIM), lambda b, g, i: (NSA_KV_HEADS + g, b, 0, 0)),
                  kv_spec(0), kv_spec(1), kv_spec(2), kv_spec(3)],
        out_specs=pl.BlockSpec((Q_BLOCK, gw), lambda b, g, i: (b * n_qb + i, g)),
        out_shape=jax.ShapeDtypeStruct((t, NSA_WIDTH), BF16),
        scratch_shapes=[pltpu.VMEM((rows, 1), F32), pltpu.VMEM((rows, 1), F32), pltpu.VMEM((rows, HEAD_DIM), F32)],
        compiler_params=_cparams("parallel", "parallel", "arbitrary"),
        name="nsa_attn",
    )(p, p, cos2, sin2, cmp_kv, cmp_kv, kvbf, kvbf, kvbf, kvbf)


def _tri(n, lower_inclusive=True):
    r = lax.broadcasted_iota(jnp.int32, (n, n), 0)
    c = lax.broadcasted_iota(jnp.int32, (n, n), 1)
    return r, c


def _softplus(x):
    return jnp.maximum(x, 0.0) + jnp.log1p(jnp.exp(-jnp.abs(x)))


def _gated_rmsnorm(o, norm_w, gate):
    y = o * lax.rsqrt(jnp.mean(o * o, axis=-1, keepdims=True) + EPS)
    return (y * norm_w) * (gate * jax.nn.sigmoid(gate))


GDN_TILE = 256
GDN_SCAN_TILE = 128


def _split_bf16(x):
    hi = x.astype(BF16)
    return hi, (x - hi.astype(F32)).astype(BF16)


def _dot3(a, b, dims=((1,), (0,))):
    return _dot(a[0], b[0], dims) + (_dot(a[0], b[1], dims) + _dot(a[1], b[0], dims))


def _chunk_cumsum(x, chunk):
    pos = lax.broadcasted_iota(jnp.int32, (x.shape[0], 1), 0) % chunk
    s = 1
    while s < chunk:
        x = x + jnp.where(pos >= s, pltpu.roll(x, s, axis=0), 0.0)
        s *= 2
    return x


def _gdn_prep_kernel(q_ref, k_ref, v_ref, ab_ref, wconv_ref, alog_ref, dtb_ref,
                     u_ref, w_ref, qd_ref, att_ref, kdt_ref, dl_ref, tail_ref):
    c_n, d, h_n, tb = GDN_CHUNK, HEAD_DIM, GDN_HEADS, GDN_TILE
    nch = tb // c_n
    w = h_n * d

    @pl.when(pl.program_id(1) == 0)
    def _():
        tail_ref[...] = jnp.zeros_like(tail_ref)

    x = jnp.concatenate([q_ref[...], k_ref[...], v_ref[...]], axis=1)
    xcat = jnp.concatenate([tail_ref[...], x], axis=0)
    wc = wconv_ref[...]
    conv = None
    for j in range(CONV_WIDTH):
        shift = CONV_WIDTH - 1 - j
        xs = xcat if shift == 0 else pltpu.roll(xcat, shift, axis=0)
        term = xs[8:8 + tb] * wc[j:j + 1, :]
        conv = term if conv is None else conv + term
    tail_ref[...] = x[tb - 8:tb]
    act = conv * jax.nn.sigmoid(conv)

    ab = ab_ref[...]
    g_all = -jnp.exp(alog_ref[...]) * _softplus(ab + dtb_ref[...])
    beta_all = jax.nn.sigmoid(ab)
    gc_all = _chunk_cumsum(g_all, c_n)
    gl_all = jnp.concatenate([jnp.broadcast_to(gc_all[(c + 1) * c_n - 1:(c + 1) * c_n], (c_n, LANE))
                              for c in range(nch)], axis=0)
    gct_all = gc_all.T
    dl_ref[...] = jnp.exp(gl_all)
    r_i, c_i = _tri(tb)
    same = (r_i // c_n) == (c_i // c_n)
    causal = same & (r_i >= c_i)
    strict = same & (r_i > c_i)
    eye = jnp.where(r_i == c_i, 1.0, 0.0)
    lane_i = lax.broadcasted_iota(jnp.int32, (d, LANE), 1)

    for h in range(h_n):
        sl = slice(h * d, (h + 1) * d)
        xq = act[:, h * d:(h + 1) * d]
        xk = act[:, w + h * d:w + (h + 1) * d]
        v = act[:, 2 * w + h * d:2 * w + (h + 1) * d]
        q = xq * lax.rsqrt(jnp.sum(xq * xq, axis=-1, keepdims=True) + EPS) * (HEAD_DIM ** -0.5)
        k = xk * lax.rsqrt(jnp.sum(xk * xk, axis=-1, keepdims=True) + EPS)
        beta = beta_all[:, h_n + h:h_n + h + 1]
        gc = gc_all[:, h:h + 1]
        gc_row = gct_all[h:h + 1, :]
        decay = jnp.where(causal, jnp.exp(jnp.where(causal, gc - gc_row, 0.0)), 0.0)
        kb = k * beta
        qk = _dot_nt(jnp.concatenate([q, kb], axis=0).astype(BF16), k.astype(BF16))
        attn = qk[:tb] * decay
        a_mat = jnp.where(strict, qk[tb:] * decay, 0.0)
        egc = jnp.exp(gc)
        rhs = jnp.concatenate([v * beta, kb * egc], axis=1)
        pw = -a_mat
        t_inv = eye + pw
        pw_s = _split_bf16(pw)
        pw = _dot3(pw_s, pw_s)
        pw_b = pw.astype(BF16)
        t_inv = t_inv + _dot3(_split_bf16(t_inv), _split_bf16(pw))
        span = 4
        while span < c_n:
            pw = _dot(pw_b, pw_b)
            pw_b = pw.astype(BF16)
            t_inv = t_inv + _dot(t_inv.astype(BF16), pw_b)
            span *= 2
        sol = _dot3(_split_bf16(t_inv), _split_bf16(rhs))
        u_ref[:, sl] = sol[:, :d]
        w_ref[:, sl] = sol[:, d:].astype(BF16)
        qd_ref[:, sl] = (q * egc).astype(BF16)
        att_ref[:, sl] = jnp.concatenate([attn[j * LANE:(j + 1) * LANE, j * LANE:(j + 1) * LANE]
                                          for j in range(tb // LANE)], axis=0).astype(BF16)
        kdt = (k * jnp.exp(gl_all[:, h:h + 1] - gc)).T
        for c in range(nch):
            blk = kdt[:, (c // 2) * LANE:(c // 2 + 1) * LANE]
            if c % 2:
                blk = pltpu.roll(blk, c_n, axis=1)
            kdt_ref[c, h] = jnp.where(lane_i < c_n, blk, 0.0).astype(BF16)


def _gdn_scan_kernel(u_ref, w_ref, qd_ref, att_ref, kdt_ref, dl_ref, z_ref, nw_ref, o_ref, state_ref):
    c_n, d, h_n = GDN_CHUNK, HEAD_DIM, GDN_HEADS

    @pl.when(pl.program_id(1) == 0)
    def _():
        state_ref[...] = jnp.zeros_like(state_ref)

    nw = nw_ref[...]
    for c in range(GDN_SCAN_TILE // c_n):
        rs = slice(c * c_n, (c + 1) * c_n)
        for h in range(h_n):
            sl = slice(h * d, (h + 1) * d)
            state = state_ref[h]
            r = _dot(jnp.concatenate([w_ref[rs, sl], qd_ref[rs, sl]], axis=0), state.astype(BF16))
            v_new = (u_ref[rs, sl] - r[:c_n]).astype(BF16)
            vv = jnp.concatenate([v_new, v_new], axis=0)
            o = r[c_n:] + _dot(att_ref[rs, sl], vv)
            state_ref[h] = state * dl_ref[c * c_n:c * c_n + 1, h:h + 1] + _dot(kdt_ref[c, h], vv)
            o_ref[rs, sl] = _gated_rmsnorm(o, nw, z_ref[rs, sl]).astype(o_ref.dtype)


def _gdn(p, w_conv, a_log, dt_bias, norm_w, batch, seq):
    t = p.shape[0]
    c_n, h_n, d = GDN_CHUNK, GDN_HEADS, HEAD_DIM
    w = h_n * d
    tb, ts = GDN_TILE, GDN_SCAN_TILE
    pad = jnp.zeros((LANE - h_n,), F32)
    alog = jnp.concatenate([a_log, pad]).reshape(1, LANE)
    dtb = jnp.concatenate([dt_bias, pad]).reshape(1, LANE)

    def whole(shape):
        return pl.BlockSpec(shape, lambda b, c: (0,) * len(shape))

    n_t = seq // tb

    def pspec(cb, width):
        return pl.BlockSpec((tb, width), lambda b, i, cb=cb, width=width: (b * n_t + i, cb * LANE // width))

    row_w = pl.BlockSpec((tb, w), lambda b, i: (b * n_t + i, 0))
    u, wm, qd, att, kdt, dl = pl.pallas_call(
        _gdn_prep_kernel,
        grid=(batch, n_t),
        in_specs=[pspec(CB_GQ, w), pspec(CB_GK, w), pspec(CB_GV, w), pspec(CB_GAB, LANE),
                  whole((CONV_WIDTH, 3 * w)), whole((1, LANE)), whole((1, LANE))],
        out_specs=[row_w, row_w, row_w, row_w,
                   pl.BlockSpec((tb // c_n, h_n, d, LANE), lambda b, i: (b * n_t + i, 0, 0, 0)),
                   pl.BlockSpec((tb, LANE), lambda b, i: (b * n_t + i, 0))],
        out_shape=[jax.ShapeDtypeStruct((t, w), F32), jax.ShapeDtypeStruct((t, w), BF16),
                   jax.ShapeDtypeStruct((t, w), BF16), jax.ShapeDtypeStruct((t, w), BF16),
                   jax.ShapeDtypeStruct((t // c_n, h_n, d, LANE), BF16), jax.ShapeDtypeStruct((t, LANE), F32)],
        scratch_shapes=[pltpu.VMEM((8, 3 * w), F32)],
        compiler_params=_cparams("parallel", "arbitrary"),
        name="gdn_prep",
    )(p, p, p, p, w_conv, alog, dtb)

    n_s = seq // ts
    row_s = pl.BlockSpec((ts, w), lambda b, i: (b * n_s + i, 0))
    return pl.pallas_call(
        _gdn_scan_kernel,
        grid=(batch, n_s),
        in_specs=[row_s, row_s, row_s, row_s,
                  pl.BlockSpec((ts // c_n, h_n, d, LANE), lambda b, i: (b * n_s + i, 0, 0, 0)),
                  pl.BlockSpec((ts, LANE), lambda b, i: (b * n_s + i, 0)),
                  pl.BlockSpec((ts, w), lambda b, i: (b * n_s + i, CB_GZ * LANE // w)),
                  whole((1, d))],
        out_specs=row_s,
        out_shape=jax.ShapeDtypeStruct((t, w), BF16),
        scratch_shapes=[pltpu.VMEM((h_n, d, d), F32)],
        compiler_params=_cparams("parallel", "arbitrary"),
        name="gdn_scan",
    )(u, wm, qd, att, kdt, dl, p, norm_w.reshape(1, d))


HGRN_SUB = 16


def _hgrn_kernel(q_ref, f_ref, i_ref, g_ref, lb_ref, nw_ref, o_ref, state_ref):
    c_n, d, h_n, sb = HGRN_CHUNK, HEAD_DIM, HGRN_HEADS, HGRN_SUB
    ci = pl.program_id(1)

    @pl.when(ci == 0)
    def _():
        state_ref[...] = jnp.zeros_like(state_ref)

    lane_i = lax.broadcasted_iota(jnp.int32, (sb, c_n), 1)
    nw = nw_ref[...]
    outs = []
    for h in range(h_n):
        sl = slice(h * d, (h + 1) * d)
        lb = lb_ref[:, sl]
        f_gate = lb + (1.0 - lb) * jax.nn.sigmoid(f_ref[:, sl])
        log_f = jnp.log(jnp.maximum(f_gate, MIN_FORGET))
        k = 1.0 - f_gate
        q = q_ref[:, sl]
        v = i_ref[:, sl]
        bcum = _chunk_cumsum(log_f, c_n)
        a_rows = []
        for sbi in range(c_n // sb):
            r0 = sbi * sb
            q_s = q[r0:r0 + sb]
            b_s = bcum[r0:r0 + sb]
            a_blk = jnp.zeros((sb, c_n), F32)
            if sbi > 0:
                b_ref0 = bcum[r0:r0 + 1]
                q_t = q_s * jnp.exp(b_s - b_ref0)
                k_t = k * jnp.exp(jnp.minimum(b_ref0 - bcum, 0.0))
                a_blk = jnp.where(lane_i < r0, _dot3(_split_bf16(q_t), _split_bf16(k_t), ((1,), (1,))), 0.0)
            for jj in range(sb):
                j = r0 + jj
                ok = lax.broadcasted_iota(jnp.int32, (sb, 1), 0) >= jj
                e = jnp.where(ok, jnp.exp(jnp.where(ok, b_s - bcum[j:j + 1], 0.0)), 0.0)
                col = jnp.sum(q_s * k[j:j + 1] * e, axis=-1, keepdims=True)
                a_blk = jnp.where(lane_i == j, col, a_blk)
            a_rows.append(a_blk)
        a_mat = jnp.concatenate(a_rows, axis=0)
        state_t = state_ref[h]
        v_bf = v.astype(BF16)
        o = _dot_nt((q * jnp.exp(bcum)).astype(BF16), state_t.astype(BF16)) + _dot(a_mat.astype(BF16), v_bf)
        b_last = bcum[c_n - 1:c_n]
        k_dec = k * jnp.exp(b_last - bcum)
        state_ref[h] = state_t * jnp.exp(b_last) + _dot(v.T.astype(BF16), k_dec.astype(BF16))
        outs.append(_gated_rmsnorm(o, nw, g_ref[:, sl]))
    o_ref[...] = jnp.concatenate(outs, axis=1).astype(o_ref.dtype)


def _hgrn(p, lb, norm_w, batch, seq):
    t = p.shape[0]
    c_n, h_n, d = HGRN_CHUNK, HGRN_HEADS, HEAD_DIM
    w = h_n * d
    n_c = seq // c_n

    def pspec(cb):
        return pl.BlockSpec((c_n, w), lambda b, c, cb=cb: (b * n_c + c, cb * LANE // w))

    return pl.pallas_call(
        _hgrn_kernel,
        grid=(batch, n_c),
        in_specs=[pspec(CB_HQ), pspec(CB_HF), pspec(CB_HI), pspec(CB_HG),
                  pl.BlockSpec((1, w), lambda b, c: (0, 0)), pl.BlockSpec((1, d), lambda b, c: (0, 0))],
        out_specs=pl.BlockSpec((c_n, w), lambda b, c: (b * n_c + c, 0)),
        out_shape=jax.ShapeDtypeStruct((t, w), BF16),
        scratch_shapes=[pltpu.VMEM((h_n, d, d), F32)],
        compiler_params=_cparams("parallel", "arbitrary"),
        name="hgrn",
    )(p, p, p, p, lb.reshape(1, w), norm_w.reshape(1, d))


def _round_up(x, m):
    return -(-x // m) * m


def _pack_plan():
    offs = [int(v) for v in np.concatenate([[0], np.cumsum(np.array(IN_SIZES))])]
    plan = []
    dst = 0
    for j in (8, 9, 10, 13, 14, 15, 16, 17, 1, 2, 3, 4, 5, 6):
        plan.append((offs[j], IN_SIZES[j], dst))
        dst += IN_SIZES[j]
    plan.append((offs[7], IN_SIZES[7], CB_NGT * LANE))
    plan.append((offs[11], IN_SIZES[11] + IN_SIZES[12], CB_GAB * LANE))
    plan.append((offs[0], IN_SIZES[0], CB_NQ * LANE))
    return tuple(plan)


def _pack_kernel(w_ref, o_ref):
    for src, width, dst in _pack_plan():
        a0 = src // LANE * LANE
        off = src - a0
        span = _round_up(off + width, LANE)
        x = w_ref[:, a0:a0 + span]
        if off:
            x = pltpu.roll(x, span - off, axis=1)
        wout = _round_up(width, LANE)
        x = x[:, :wout]
        if width % LANE:
            x = jnp.where(lax.broadcasted_iota(jnp.int32, x.shape, 1) < width, x, 0.0)
        o_ref[:, dst:dst + wout] = x.astype(o_ref.dtype)


def _pack_w_in(w_in, layer, tr=256):
    _, k, n = w_in.shape
    return pl.pallas_call(
        _pack_kernel,
        grid=(k // tr,),
        in_specs=[pl.BlockSpec((None, tr, _round_up(n, LANE)), lambda i: (layer, i, 0))],
        out_specs=pl.BlockSpec((tr, P_WIDTH), lambda i: (i, 0)),
        out_shape=jax.ShapeDtypeStruct((k, P_WIDTH), BF16),
        compiler_params=_cparams("parallel"),
        name="pack_w_in",
    )(w_in)


def _rope_tables(seq):
    inv = 1.0 / (ROPE_THETA ** (jnp.arange(0, HEAD_DIM, 2, dtype=F32) / HEAD_DIM))
    ang = jnp.arange(seq, dtype=F32)[:, None] * inv[None, :]
    cos, sin = jnp.cos(ang), jnp.sin(ang)
    return jnp.concatenate([cos, cos], axis=1), jnp.concatenate([-sin, sin], axis=1)


def _mixers(p, batch, seq, layer_params):
    (pe_k, pe_v, ck1, ck2, cv1, cv2, conv_w, a_log, dt_bias, gdn_norm, lb, hgrn_norm, cos2, sin2) = layer_params
    cv, kvbf = _nsa_prep(p, cos2, sin2, seq)
    cmp_kv = _nsa_compress(cv, jnp.stack([pe_k, pe_v]), jnp.stack([ck1, cv1]), jnp.stack([ck2, cv2]), batch, seq)
    y_nsa = _nsa_attn(p, cmp_kv, kvbf, cos2, sin2, batch, seq)
    y_gdn = _gdn(p, conv_w, a_log, dt_bias, gdn_norm, batch, seq)
    y_hgrn = _hgrn(p, lb, hgrn_norm, batch, seq)
    return jnp.concatenate([y_nsa, y_gdn, y_hgrn], axis=1)


def kernel(x, ffn1_norm, ffn1_gate, ffn1_up, ffn1_down, mix_norm, w_in, w_out, nsa_pe_k, nsa_pe_v, nsa_ck1, nsa_ck2, nsa_cv1, nsa_cv2, gdn_conv, gdn_a_log, gdn_dt_bias, gdn_norm, hgrn_lb, hgrn_norm, ffn2_norm, ffn2_gate, ffn2_up, ffn2_down, final_norm):
    batch, seq, dm = x.shape
    depth = w_in.shape[0]
    cos2, sin2 = _rope_tables(seq)
    p_lb = jax.nn.softmax(hgrn_lb.astype(F32), axis=0)
    lb_all = jnp.cumsum(p_lb, axis=0) - p_lb[0:1]
    xt = x.reshape(batch * seq, dm)

    def ffn(xt, norm, wg, wu, wd, l):
        h = _rmsnorm(xt, norm[l], BF16)
        act = _ffn_up(h, wg, wu, l)
        return _mm_res(act, wd[l].astype(BF16), xt, 0.5)

    for l in range(depth):
        xt = ffn(xt, ffn1_norm, ffn1_gate, ffn1_up, ffn1_down, l)
        h = _rmsnorm(xt, mix_norm[l], BF16)
        p = _mm(h, _pack_w_in(w_in, l), F32)
        y = _mixers(p, batch, seq, (nsa_pe_k[l], nsa_pe_v[l], nsa_ck1[l], nsa_ck2[l], nsa_cv1[l], nsa_cv2[l],
                                    gdn_conv[l], gdn_a_log[l], gdn_dt_bias[l], gdn_norm[l], lb_all[l], hgrn_norm[l],
                                    cos2, sin2))
        xt = _mm_res(y, w_out[l].astype(BF16), xt, 1.0)
        xt = ffn(xt, ffn2_norm, ffn2_gate, ffn2_up, ffn2_down, l)
    return _rmsnorm(xt, final_norm, F32).reshape(batch, seq, dm)
```

```python
import functools

import jax
import jax.numpy as jnp
import numpy as np
from jax import lax
from jax.experimental import pallas as pl
from jax.experimental.pallas import tpu as pltpu

F32 = jnp.float32
BF16 = jnp.bfloat16
HIGHEST = lax.Precision.HIGHEST

D_MODEL = 2048
HEAD_DIM = 128
ROPE_THETA = 10000.0
EPS = 1e-6
MASK_VALUE = -1e30
FORCE_VALUE = 1e9
MIN_FORGET = 1e-6
NSA_HEADS = 6
NSA_KV_HEADS = 2
NSA_REP = NSA_HEADS // NSA_KV_HEADS
CMP_BLOCK = 32
CMP_STRIDE = 16
SEL_BLOCK = 64
N_SEL = 16
WINDOW = 512
Q_BLOCK = 128
GDN_HEADS = 5
CONV_WIDTH = 4
GDN_CHUNK = 64
HGRN_HEADS = 5
HGRN_CHUNK = 64
D_FF = 5632

NSA_WIDTH = NSA_HEADS * HEAD_DIM
KV_WIDTH = NSA_KV_HEADS * HEAD_DIM
GDN_WIDTH = GDN_HEADS * HEAD_DIM
HGRN_WIDTH = HGRN_HEADS * HEAD_DIM
MIX_WIDTH = NSA_WIDTH + GDN_WIDTH + HGRN_WIDTH
IN_SIZES = (NSA_WIDTH, KV_WIDTH, KV_WIDTH, KV_WIDTH, KV_WIDTH, KV_WIDTH, KV_WIDTH, 3 * NSA_HEADS,
            GDN_WIDTH, GDN_WIDTH, GDN_WIDTH, GDN_HEADS, GDN_HEADS, GDN_WIDTH,
            HGRN_WIDTH, HGRN_WIDTH, HGRN_WIDTH, HGRN_WIDTH)

SEL_TILE = 512
LANE = 128
VMEM_LIMIT = 48 * 1024 * 1024

CB_GQ = 0
CB_GK = 5
CB_GV = 10
CB_GZ = 15
CB_HQ = 20
CB_HF = 25
CB_HI = 30
CB_HG = 35
CB_NKC = 40
CB_NVC = 42
CB_NKS = 44
CB_NVS = 46
CB_NKW = 48
CB_NVW = 50
CB_NGT = 52
CB_GAB = 53
CB_NQ = 54
P_BLOCKS = 60
P_WIDTH = P_BLOCKS * LANE


def _cparams(*sem):
    return pltpu.CompilerParams(dimension_semantics=sem, vmem_limit_bytes=VMEM_LIMIT)


def _dot(a, b, dims=((1,), (0,)), precision=None):
    return lax.dot_general(a, b, (dims, ((), ())), precision=precision, preferred_element_type=F32)


def _dot_nt(a, b, precision=None):
    return _dot(a, b, ((1,), (1,)), precision)


def _dot_tn(a, b, precision=None):
    return _dot(a, b, ((0,), (0,)), precision)


def _rmsnorm_kernel(x_ref, g_ref, o_ref):
    x = x_ref[...]
    y = x * lax.rsqrt(jnp.mean(x * x, axis=-1, keepdims=True) + EPS)
    o_ref[...] = (y * g_ref[...]).astype(o_ref.dtype)


def _rmsnorm(x, g, out_dtype, tm=512):
    t, d = x.shape
    return pl.pallas_call(
        _rmsnorm_kernel,
        grid=(t // tm,),
        in_specs=[pl.BlockSpec((tm, d), lambda i: (i, 0)), pl.BlockSpec((1, d), lambda i: (0, 0))],
        out_specs=pl.BlockSpec((tm, d), lambda i: (i, 0)),
        out_shape=jax.ShapeDtypeStruct((t, d), out_dtype),
        compiler_params=_cparams("parallel"),
        name="rmsnorm",
    )(x, g.reshape(1, d))


def _ffn_up_kernel(h_ref, wg_ref, wu_ref, o_ref, wg_bf, wu_bf):
    @pl.when(pl.program_id(1) == 0)
    def _():
        wg_bf[...] = wg_ref[...].astype(BF16)
        wu_bf[...] = wu_ref[...].astype(BF16)

    h = h_ref[...]
    a = jnp.dot(h, wg_bf[...], preferred_element_type=F32)
    b = jnp.dot(h, wu_bf[...], preferred_element_type=F32)
    o_ref[...] = (a * jax.nn.sigmoid(a) * b).astype(o_ref.dtype)


def _ffn_up(h, w_gate, w_up, layer, tm=1024, tn=512):
    t, k = h.shape
    n = w_gate.shape[2]
    wspec = pl.BlockSpec((None, k, tn), lambda j, i: (layer, 0, j))
    return pl.pallas_call(
        _ffn_up_kernel,
        grid=(n // tn, t // tm),
        in_specs=[pl.BlockSpec((tm, k), lambda j, i: (i, 0)), wspec, wspec],
        out_specs=pl.BlockSpec((tm, tn), lambda j, i: (i, j)),
        out_shape=jax.ShapeDtypeStruct((t, n), BF16),
        scratch_shapes=[pltpu.VMEM((k, tn), BF16), pltpu.VMEM((k, tn), BF16)],
        compiler_params=_cparams("parallel", "arbitrary"),
        name="ffn_up",
    )(h, w_gate, w_up)


def _mm_kernel(a_ref, b_ref, o_ref):
    o_ref[...] = jnp.dot(a_ref[...], b_ref[...], preferred_element_type=F32).astype(o_ref.dtype)


def _mm(a, b, out_dtype, tm=1024, tn=512):
    t, k = a.shape
    n = b.shape[1]
    return pl.pallas_call(
        _mm_kernel,
        grid=(t // tm, n // tn),
        in_specs=[pl.BlockSpec((tm, k), lambda i, j: (i, 0)), pl.BlockSpec((k, tn), lambda i, j: (0, j))],
        out_specs=pl.BlockSpec((tm, tn), lambda i, j: (i, j)),
        out_shape=jax.ShapeDtypeStruct((t, n), out_dtype),
        compiler_params=_cparams("parallel", "parallel"),
        name="proj_in",
    )(a, b)


def _mm_res_kernel(a_ref, b_ref, r_ref, o_ref, *, scale):
    o_ref[...] = r_ref[...] + scale * jnp.dot(a_ref[...], b_ref[...], preferred_element_type=F32)


def _mm_res(a, b, res, scale, tm=1024, tn=512):
    t, kdim = a.shape
    n = b.shape[1]
    return pl.pallas_call(
        functools.partial(_mm_res_kernel, scale=scale),
        grid=(t // tm, n // tn),
        in_specs=[pl.BlockSpec((tm, kdim), lambda i, j: (i, 0)),
                  pl.BlockSpec((kdim, tn), lambda i, j: (0, j)),
                  pl.BlockSpec((tm, tn), lambda i, j: (i, j))],
        out_specs=pl.BlockSpec((tm, tn), lambda i, j: (i, j)),
        out_shape=jax.ShapeDtypeStruct((t, n), F32),
        compiler_params=_cparams("parallel", "parallel"),
        name="mm_residual",
    )(a, b, res)


def _rope(x, cos2, sin2):
    return x * cos2 + pltpu.roll(x, HEAD_DIM // 2, axis=1) * sin2


def _nsa_prep_kernel(kc_ref, vc_ref, ks_ref, vs_ref, kw_ref, vw_ref, cos_ref, sin_ref, cv_ref, kv_ref):
    cos2 = cos_ref[...]
    sin2 = sin_ref[...]
    for g in range(NSA_KV_HEADS):
        sl = slice(g * HEAD_DIM, (g + 1) * HEAD_DIM)
        cv_ref[g] = _rope(kc_ref[:, sl], cos2, sin2)
        cv_ref[NSA_KV_HEADS + g] = vc_ref[:, sl]
        for j, (ref, roped) in enumerate(((ks_ref, True), (vs_ref, False), (kw_ref, True), (vw_ref, False))):
            x = ref[:, sl]
            if roped:
                x = _rope(x, cos2, sin2)
            c0 = (j * NSA_KV_HEADS + g) * HEAD_DIM
            kv_ref[:, c0:c0 + HEAD_DIM] = x.astype(BF16)


def _nsa_prep(p, cos2, sin2, seq, tile=512):
    t = p.shape[0]
    per_seq = seq // tile
    kvw = KV_WIDTH

    def pspec(cb):
        return pl.BlockSpec((tile, kvw), lambda i, cb=cb: (i, cb * LANE // kvw))

    tab = pl.BlockSpec((tile, HEAD_DIM), lambda i: (i % per_seq, 0))
    return pl.pallas_call(
        _nsa_prep_kernel,
        grid=(t // tile,),
        in_specs=[pspec(CB_NKC), pspec(CB_NVC), pspec(CB_NKS), pspec(CB_NVS), pspec(CB_NKW), pspec(CB_NVW), tab, tab],
        out_specs=[pl.BlockSpec((2 * NSA_KV_HEADS, tile, HEAD_DIM), lambda i: (0, i, 0)),
                   pl.BlockSpec((tile, 4 * kvw), lambda i: (i, 0))],
        out_shape=[jax.ShapeDtypeStruct((2 * NSA_KV_HEADS, t, HEAD_DIM), F32),
                   jax.ShapeDtypeStruct((t, 4 * kvw), BF16)],
        compiler_params=_cparams("parallel"),
        name="nsa_prep",
    )(p, p, p, p, p, p, cos2, sin2)


def _nsa_compress_kernel(t_ref, pe_ref, w1_ref, w2_ref, o_ref):
    x = t_ref[...]
    half = CMP_STRIDE * HEAD_DIM
    nr = x.shape[0]
    u = _dot(x + pe_ref[0:1, :], w1_ref[0:half, :], precision=HIGHEST)
    v = _dot(x + pe_ref[1:2, :], w1_ref[half:2 * half, :], precision=HIGHEST)
    pre = u + pltpu.roll(v, nr - 1, axis=0)
    o_ref[...] = _dot(jax.nn.gelu(pre), w2_ref[...], precision=HIGHEST)


def _nsa_compress(cv, pe, w1, w2, batch, seq):
    nr = seq // CMP_STRIDE
    half = CMP_STRIDE * HEAD_DIM
    t2 = cv.reshape(2 * NSA_KV_HEADS, batch, nr, half)
    return pl.pallas_call(
        _nsa_compress_kernel,
        grid=(2 * NSA_KV_HEADS, batch),
        in_specs=[pl.BlockSpec((None, None, nr, half), lambda c, b: (c, b, 0, 0)),
                  pl.BlockSpec((None, 2, half), lambda c, b: (c // NSA_KV_HEADS, 0, 0)),
                  pl.BlockSpec((None, 2 * half, HEAD_DIM), lambda c, b: (c // NSA_KV_HEADS, 0, 0)),
                  pl.BlockSpec((None, HEAD_DIM, HEAD_DIM), lambda c, b: (c // NSA_KV_HEADS, 0, 0))],
        out_specs=pl.BlockSpec((None, None, nr, HEAD_DIM), lambda c, b: (c, b, 0, 0)),
        out_shape=jax.ShapeDtypeStruct((2 * NSA_KV_HEADS, batch, nr, HEAD_DIM), F32),
        compiler_params=_cparams("parallel", "parallel"),
        name="nsa_compress",
    )(t2, pe.reshape(2, 2, half), w1, w2)


def _masked_softmax(s, mask):
    s = jnp.where(mask, s, MASK_VALUE)
    m = jnp.max(s, axis=-1, keepdims=True)
    e = jnp.where(mask, jnp.exp(s - m), 0.0)
    return e / jnp.maximum(jnp.sum(e, axis=-1, keepdims=True), 1e-30)


def _nsa_attn_kernel(q_ref, gt_ref, cos_ref, sin_ref, kc_ref, vc_ref, ks_ref, vs_ref, kw_ref, vw_ref,
                     o_ref, m_ref, l_ref, acc_ref, *, seq):
    g = pl.program_id(1)
    i = pl.program_id(2)
    qn, r_n, d = Q_BLOCK, NSA_REP, HEAD_DIM
    rows = r_n * qn
    n_blk = seq // SEL_BLOCK
    n_sel = min(N_SEL, n_blk)
    ncr = seq // CMP_STRIDE

    cos2 = cos_ref[...]
    sin2 = sin_ref[...]
    qb = q_ref[...]
    q = jnp.concatenate([_rope(qb[:, r * d:(r + 1) * d], cos2, sin2) * (HEAD_DIM ** -0.5) for r in range(r_n)],
                        axis=0)
    q_bf = q.astype(BF16)
    t0 = i * qn
    t_col = t0 + lax.broadcasted_iota(jnp.int32, (qn, 1), 0)
    t_rows = jnp.concatenate([t_col] * r_n, axis=0)

    kc = kc_ref[...]
    vc = vc_ref[...]
    s_c = _dot3(_split_bf16(q), _split_bf16(kc), ((1,), (1,)))
    cmp_end = lax.broadcasted_iota(jnp.int32, (1, ncr), 1) * CMP_STRIDE + (CMP_BLOCK - 1)
    p_c = _masked_softmax(s_c, cmp_end <= t_rows)
    o_c = _dot(p_c.astype(BF16), vc.astype(BF16))

    pc_sum = p_c[0:qn]
    for r in range(1, r_n):
        pc_sum = pc_sum + p_c[r * qn:(r + 1) * qn]
    m_i = lax.broadcasted_iota(jnp.int32, (n_blk, ncr), 0)
    c_i = lax.broadcasted_iota(jnp.int32, (n_blk, ncr), 1)
    ov = (jnp.minimum(c_i * CMP_STRIDE + CMP_BLOCK, m_i * SEL_BLOCK + SEL_BLOCK)
          - jnp.maximum(c_i * CMP_STRIDE, m_i * SEL_BLOCK))
    ov_t = jnp.where(c_i < ncr - 1, jnp.maximum(ov, 0).astype(F32) / CMP_BLOCK, 0.0)
    ov_bf = ov_t.astype(BF16)
    pc_hi, pc_lo = _split_bf16(pc_sum)
    imp = _dot_nt(ov_bf, pc_hi) + _dot_nt(ov_bf, pc_lo)
    blk = lax.broadcasted_iota(jnp.int32, (n_blk, qn), 0)
    t_row = t0 + lax.broadcasted_iota(jnp.int32, (n_blk, qn), 1)
    cur = t_row // SEL_BLOCK
    forced = (blk == 0) | (blk == cur) | (blk == cur - 1)
    future = blk * SEL_BLOCK > t_row
    imp = jnp.where(forced, FORCE_VALUE, jnp.where(future, -FORCE_VALUE, imp))
    rank = jnp.zeros((n_blk, qn), F32)
    for j in range(n_blk):
        row = imp[j:j + 1, :]
        beats = (row > imp) | ((row == imp) & (blk > j))
        rank = rank + jnp.where(beats, 1.0, 0.0)
    sel_t = jnp.where(rank < n_sel, 1.0, 0.0)
    sel = sel_t.T.astype(BF16)

    m_ref[...] = jnp.full((rows, 1), MASK_VALUE, F32)
    l_ref[...] = jnp.zeros((rows, 1), F32)
    acc_ref[...] = jnp.zeros((rows, d), F32)
    n_tiles = (t0 + qn + SEL_TILE - 1) // SEL_TILE

    def sel_step(kt, carry):
        k0 = pl.multiple_of(kt * SEL_TILE, SEL_TILE)
        s = _dot_nt(q_bf, ks_ref[pl.ds(k0, SEL_TILE), :])
        kpos = k0 + lax.broadcasted_iota(jnp.int32, (1, SEL_TILE), 1)
        e_blk = lax.broadcasted_iota(jnp.int32, (n_blk, SEL_TILE), 0)
        e_key = k0 + lax.broadcasted_iota(jnp.int32, (n_blk, SEL_TILE), 1)
        expand = jnp.where(e_blk == e_key // SEL_BLOCK, 1.0, 0.0).astype(BF16)
        chosen = _dot(sel, expand)
        ok_q = (chosen > 0.5) & (kpos <= t_col)
        ok = jnp.concatenate([ok_q] * r_n, axis=0)
        s = jnp.where(ok, s, MASK_VALUE)
        m_old = m_ref[...]
        m_new = jnp.maximum(m_old, jnp.max(s, axis=-1, keepdims=True))
        alpha = jnp.exp(m_old - m_new)
        p = jnp.exp(s - m_new)
        l_ref[...] = alpha * l_ref[...] + jnp.sum(p, axis=-1, keepdims=True)
        acc_ref[...] = alpha * acc_ref[...] + _dot(p.astype(BF16), vs_ref[pl.ds(k0, SEL_TILE), :])
        m_ref[...] = m_new
        return carry

    lax.fori_loop(0, n_tiles, sel_step, 0)
    o_s = acc_ref[...] / jnp.maximum(l_ref[...], 1e-30)

    wlen = WINDOW + qn
    w0 = pl.multiple_of(jnp.maximum(t0 - WINDOW, 0), qn)
    kw = kw_ref[pl.ds(w0, wlen), :]
    vw = vw_ref[pl.ds(w0, wlen), :]
    s_w = _dot_nt(q_bf, kw)
    wpos = w0 + lax.broadcasted_iota(jnp.int32, (1, wlen), 1)
    p_w = _masked_softmax(s_w, (wpos <= t_rows) & (wpos > t_rows - WINDOW))
    o_w = _dot(p_w.astype(BF16), vw)

    gates = jax.nn.sigmoid(gt_ref[...])
    outs = []
    for r in range(r_n):
        sl = slice(r * qn, (r + 1) * qn)
        o_r = 0.0
        for c, o_b in enumerate((o_c, o_s, o_w)):
            lane = r * 3 + c
            gcol = jnp.where(g == 0, gates[:, lane:lane + 1], gates[:, 3 * r_n + lane:3 * r_n + lane + 1])
            o_r = o_r + gcol * o_b[sl]
        outs.append(o_r)
    o_ref[...] = jnp.concatenate(outs, axis=1).astype(o_ref.dtype)


def _nsa_attn(p, cmp_kv, kvbf, cos2, sin2, batch, seq):
    t = p.shape[0]
    n_qb = seq // Q_BLOCK
    ncr = seq // CMP_STRIDE
    gw = NSA_REP * HEAD_DIM
    rows = NSA_REP * Q_BLOCK

    def kv_spec(j):
        return pl.BlockSpec((seq, HEAD_DIM), lambda b, g, i, j=j: (b, j * NSA_KV_HEADS + g))

    return pl.pallas_call(
        functools.partial(_nsa_attn_kernel, seq=seq),
        grid=(batch, NSA_KV_HEADS, n_qb),
        in_specs=[pl.BlockSpec((Q_BLOCK, gw), lambda b, g, i: (b * n_qb + i, CB_NQ // NSA_REP + g)),
                  pl.BlockSpec((Q_BLOCK, LANE), lambda b, g, i: (b * n_qb + i, CB_NGT)),
                  pl.BlockSpec((Q_BLOCK, HEAD_DIM), lambda b, g, i: (i, 0)),
                  pl.BlockSpec((Q_BLOCK, HEAD_DIM), lambda b, g, i: (i, 0)),
                  pl.BlockSpec((None, None, ncr, HEAD_DIM), lambda b, g, i: (g, b, 0, 0)),
                  pl.BlockSpec((None, None, ncr, HEAD_DIM), lambda b, g, i: (NSA_KV_HEADS + g, b, 0, 0)),
                  kv_spec(0), kv_spec(1), kv_spec(2), kv_spec(3)],
        out_specs=pl.BlockSpec((Q_BLOCK, gw), lambda b, g, i: (b * n_qb + i, g)),
        out_shape=jax.ShapeDtypeStruct((t, NSA_WIDTH), BF16),
        scratch_shapes=[pltpu.VMEM((rows, 1), F32), pltpu.VMEM((rows, 1), F32), pltpu.VMEM((rows, HEAD_DIM), F32)],
        compiler_params=_cparams("parallel", "parallel", "arbitrary"),
        name="nsa_attn",
    )(p, p, cos2, sin2, cmp_kv, cmp_kv, kvbf, kvbf, kvbf, kvbf)


def _tri(n, lower_inclusive=True):
    r = lax.broadcasted_iota(jnp.int32, (n, n), 0)
    c = lax.broadcasted_iota(jnp.int32, (n, n), 1)
    return r, c


def _softplus(x):
    return jnp.maximum(x, 0.0) + jnp.log1p(jnp.exp(-jnp.abs(x)))


def _gated_rmsnorm(o, norm_w, gate):
    y = o * lax.rsqrt(jnp.mean(o * o, axis=-1, keepdims=True) + EPS)
    return (y * norm_w) * (gate * jax.nn.sigmoid(gate))


GDN_TILE = 256
GDN_SCAN_TILE = 128


def _split_bf16(x):
    hi = x.astype(BF16)
    return hi, (x - hi.astype(F32)).astype(BF16)


def _dot3(a, b, dims=((1,), (0,))):
    return _dot(a[0], b[0], dims) + (_dot(a[0], b[1], dims) + _dot(a[1], b[0], dims))


def _chunk_cumsum(x, chunk):
    pos = lax.broadcasted_iota(jnp.int32, (x.shape[0], 1), 0) % chunk
    s = 1
    while s < chunk:
        x = x + jnp.where(pos >= s, pltpu.roll(x, s, axis=0), 0.0)
        s *= 2
    return x


def _gdn_prep_kernel(q_ref, k_ref, v_ref, ab_ref, wconv_ref, alog_ref, dtb_ref,
                     u_ref, w_ref, qd_ref, att_ref, kdt_ref, dl_ref, tail_ref):
    c_n, d, h_n, tb = GDN_CHUNK, HEAD_DIM, GDN_HEADS, GDN_TILE
    nch = tb // c_n
    w = h_n * d

    @pl.when(pl.program_id(1) == 0)
    def _():
        tail_ref[...] = jnp.zeros_like(tail_ref)

    x = jnp.concatenate([q_ref[...], k_ref[...], v_ref[...]], axis=1)
    xcat = jnp.concatenate([tail_ref[...], x], axis=0)
    wc = wconv_ref[...]
    conv = None
    for j in range(CONV_WIDTH):
        shift = CONV_WIDTH - 1 - j
        xs = xcat if shift == 0 else pltpu.roll(xcat, shift, axis=0)
        term = xs[8:8 + tb] * wc[j:j + 1, :]
        conv = term if conv is None else conv + term
    tail_ref[...] = x[tb - 8:tb]
    act = conv * jax.nn.sigmoid(conv)

    ab = ab_ref[...]
    g_all = -jnp.exp(alog_ref[...]) * _softplus(ab + dtb_ref[...])
    beta_all = jax.nn.sigmoid(ab)
    gc_all = _chunk_cumsum(g_all, c_n)
    gl_all = jnp.concatenate([jnp.broadcast_to(gc_all[(c + 1) * c_n - 1:(c + 1) * c_n], (c_n, LANE))
                              for c in range(nch)], axis=0)
    gct_all = gc_all.T
    dl_ref[...] = jnp.exp(gl_all)
    r_i, c_i = _tri(tb)
    same = (r_i // c_n) == (c_i // c_n)
    causal = same & (r_i >= c_i)
    strict = same & (r_i > c_i)
    eye = jnp.where(r_i == c_i, 1.0, 0.0)
    lane_i = lax.broadcasted_iota(jnp.int32, (d, LANE), 1)

    for h in range(h_n):
        sl = slice(h * d, (h + 1) * d)
        xq = act[:, h * d:(h + 1) * d]
        xk = act[:, w + h * d:w + (h + 1) * d]
        v = act[:, 2 * w + h * d:2 * w + (h + 1) * d]
        q = xq * lax.rsqrt(jnp.sum(xq * xq, axis=-1, keepdims=True) + EPS) * (HEAD_DIM ** -0.5)
        k = xk * lax.rsqrt(jnp.sum(xk * xk, axis=-1, keepdims=True) + EPS)
        beta = beta_all[:, h_n + h:h_n + h + 1]
        gc = gc_all[:, h:h + 1]
        gc_row = gct_all[h:h + 1, :]
        decay = jnp.where(causal, jnp.exp(jnp.where(causal, gc - gc_row, 0.0)), 0.0)
        kb = k * beta
        qk = _dot_nt(jnp.concatenate([q, kb], axis=0).astype(BF16), k.astype(BF16))
        attn = qk[:tb] * decay
        a_mat = jnp.where(strict, qk[tb:] * decay, 0.0)
        egc = jnp.exp(gc)
        rhs = jnp.concatenate([v * beta, kb * egc], axis=1)
        pw = -a_mat
        t_inv = eye + pw
        pw_s = _split_bf16(pw)
        pw = _dot3(pw_s, pw_s)
        pw_b = pw.astype(BF16)
        t_inv = t_inv + _dot3(_split_bf16(t_inv), _split_bf16(pw))
        span = 4
        while span < c_n:
            pw = _dot(pw_b, pw_b)
            pw_b = pw.astype(BF16)
            t_inv = t_inv + _dot(t_inv.astype(BF16), pw_b)
            span *= 2
        sol = _dot3(_split_bf16(t_inv), _split_bf16(rhs))
        u_ref[:, sl] = sol[:, :d]
        w_ref[:, sl] = sol[:, d:].astype(BF16)
        qd_ref[:, sl] = (q * egc).astype(BF16)
        att_ref[:, sl] = jnp.concatenate([attn[j * LANE:(j + 1) * LANE, j * LANE:(j + 1) * LANE]
                                          for j in range(tb // LANE)], axis=0).astype(BF16)
        kdt = (k * jnp.exp(gl_all[:, h:h + 1] - gc)).T
        for c in range(nch):
            blk = kdt[:, (c // 2) * LANE:(c // 2 + 1) * LANE]
            if c % 2:
                blk = pltpu.roll(blk, c_n, axis=1)
            kdt_ref[c, h] = jnp.where(lane_i < c_n, blk, 0.0).astype(BF16)


def _gdn_scan_kernel(u_ref, w_ref, qd_ref, att_ref, kdt_ref, dl_ref, z_ref, nw_ref, o_ref, state_ref):
    c_n, d, h_n = GDN_CHUNK, HEAD_DIM, GDN_HEADS
    n_b = u_ref.shape[0]

    @pl.when(pl.program_id(0) == 0)
    def _():
        state_ref[...] = jnp.zeros_like(state_ref)

    nw = nw_ref[...]
    for c in range(GDN_SCAN_TILE // c_n):
        rs = slice(c * c_n, (c + 1) * c_n)
        for b in range(n_b):
            for h in range(h_n):
                sl = slice(h * d, (h + 1) * d)
                state = state_ref[b * h_n + h]
                r = _dot(jnp.concatenate([w_ref[b, rs, sl], qd_ref[b, rs, sl]], axis=0), state.astype(BF16))
                v_new = (u_ref[b, rs, sl] - r[:c_n]).astype(BF16)
                vv = jnp.concatenate([v_new, v_new], axis=0)
                o = r[c_n:] + _dot(att_ref[b, rs, sl], vv)
                state_ref[b * h_n + h] = (state * dl_ref[b, c * c_n:c * c_n + 1, h:h + 1]
                                          + _dot(kdt_ref[b, c, h], vv))
                o_ref[b, rs, sl] = _gated_rmsnorm(o, nw, z_ref[b, rs, sl]).astype(o_ref.dtype)


def _gdn(p, w_conv, a_log, dt_bias, norm_w, batch, seq):
    t = p.shape[0]
    c_n, h_n, d = GDN_CHUNK, GDN_HEADS, HEAD_DIM
    w = h_n * d
    tb, ts = GDN_TILE, GDN_SCAN_TILE
    pad = jnp.zeros((LANE - h_n,), F32)
    alog = jnp.concatenate([a_log, pad]).reshape(1, LANE)
    dtb = jnp.concatenate([dt_bias, pad]).reshape(1, LANE)

    def whole(shape):
        return pl.BlockSpec(shape, lambda b, c: (0,) * len(shape))

    n_t = seq // tb

    def pspec(cb, width):
        return pl.BlockSpec((tb, width), lambda b, i, cb=cb, width=width: (b * n_t + i, cb * LANE // width))

    row_w = pl.BlockSpec((tb, w), lambda b, i: (b * n_t + i, 0))
    u, wm, qd, att, kdt, dl = pl.pallas_call(
        _gdn_prep_kernel,
        grid=(batch, n_t),
        in_specs=[pspec(CB_GQ, w), pspec(CB_GK, w), pspec(CB_GV, w), pspec(CB_GAB, LANE),
                  whole((CONV_WIDTH, 3 * w)), whole((1, LANE)), whole((1, LANE))],
        out_specs=[row_w, row_w, row_w, row_w,
                   pl.BlockSpec((tb // c_n, h_n, d, LANE), lambda b, i: (b * n_t + i, 0, 0, 0)),
                   pl.BlockSpec((tb, LANE), lambda b, i: (b * n_t + i, 0))],
        out_shape=[jax.ShapeDtypeStruct((t, w), F32), jax.ShapeDtypeStruct((t, w), BF16),
                   jax.ShapeDtypeStruct((t, w), BF16), jax.ShapeDtypeStruct((t, w), BF16),
                   jax.ShapeDtypeStruct((t // c_n, h_n, d, LANE), BF16), jax.ShapeDtypeStruct((t, LANE), F32)],
        scratch_shapes=[pltpu.VMEM((8, 3 * w), F32)],
        compiler_params=_cparams("parallel", "arbitrary"),
        name="gdn_prep",
    )(p, p, p, p, w_conv, alog, dtb)

    def per_batch(a):
        return a.reshape((batch, a.shape[0] // batch) + a.shape[1:])

    row_s = pl.BlockSpec((batch, ts, w), lambda i: (0, i, 0))
    y = pl.pallas_call(
        _gdn_scan_kernel,
        grid=(seq // ts,),
        in_specs=[row_s, row_s, row_s, row_s,
                  pl.BlockSpec((batch, ts // c_n, h_n, d, LANE), lambda i: (0, i, 0, 0, 0)),
                  pl.BlockSpec((batch, ts, LANE), lambda i: (0, i, 0)),
                  pl.BlockSpec((batch, ts, w), lambda i: (0, i, CB_GZ * LANE // w)),
                  pl.BlockSpec((1, d), lambda i: (0, 0))],
        out_specs=row_s,
        out_shape=jax.ShapeDtypeStruct((batch, seq, w), BF16),
        scratch_shapes=[pltpu.VMEM((batch * h_n, d, d), F32)],
        compiler_params=_cparams("arbitrary"),
        name="gdn_scan",
    )(per_batch(u), per_batch(wm), per_batch(qd), per_batch(att), per_batch(kdt), per_batch(dl), per_batch(p),
      norm_w.reshape(1, d))
    return y.reshape(t, w)


HGRN_SUB = 16


def _hgrn_kernel(q_ref, f_ref, i_ref, g_ref, lb_ref, nw_ref, o_ref, state_ref):
    c_n, d, h_n, sb = HGRN_CHUNK, HEAD_DIM, HGRN_HEADS, HGRN_SUB
    ci = pl.program_id(1)

    @pl.when(ci == 0)
    def _():
        state_ref[...] = jnp.zeros_like(state_ref)

    lane_i = lax.broadcasted_iota(jnp.int32, (sb, c_n), 1)
    nw = nw_ref[...]
    outs = []
    for h in range(h_n):
        sl = slice(h * d, (h + 1) * d)
        lb = lb_ref[:, sl]
        f_gate = lb + (1.0 - lb) * jax.nn.sigmoid(f_ref[:, sl])
        log_f = jnp.log(jnp.maximum(f_gate, MIN_FORGET))
        k = 1.0 - f_gate
        q = q_ref[:, sl]
        v = i_ref[:, sl]
        bcum = _chunk_cumsum(log_f, c_n)
        a_rows = []
        for sbi in range(c_n // sb):
            r0 = sbi * sb
            q_s = q[r0:r0 + sb]
            b_s = bcum[r0:r0 + sb]
            a_blk = jnp.zeros((sb, c_n), F32)
            if sbi > 0:
                b_ref0 = bcum[r0:r0 + 1]
                q_t = q_s * jnp.exp(b_s - b_ref0)
                k_t = k * jnp.exp(jnp.minimum(b_ref0 - bcum, 0.0))
                a_blk = jnp.where(lane_i < r0, _dot3(_split_bf16(q_t), _split_bf16(k_t), ((1,), (1,))), 0.0)
            for jj in range(sb):
                j = r0 + jj
                ok = lax.broadcasted_iota(jnp.int32, (sb, 1), 0) >= jj
                e = jnp.where(ok, jnp.exp(jnp.where(ok, b_s - bcum[j:j + 1], 0.0)), 0.0)
                col = jnp.sum(q_s * k[j:j + 1] * e, axis=-1, keepdims=True)
                a_blk = jnp.where(lane_i == j, col, a_blk)
            a_rows.append(a_blk)
        a_mat = jnp.concatenate(a_rows, axis=0)
        state_t = state_ref[h]
        v_bf = v.astype(BF16)
        o = _dot_nt((q * jnp.exp(bcum)).astype(BF16), state_t.astype(BF16)) + _dot(a_mat.astype(BF16), v_bf)
        b_last = bcum[c_n - 1:c_n]
        k_dec = k * jnp.exp(b_last - bcum)
        state_ref[h] = state_t * jnp.exp(b_last) + _dot(v.T.astype(BF16), k_dec.astype(BF16))
        outs.append(_gated_rmsnorm(o, nw, g_ref[:, sl]))
    o_ref[...] = jnp.concatenate(outs, axis=1).astype(o_ref.dtype)


def _hgrn(p, lb, norm_w, batch, seq):
    t = p.shape[0]
    c_n, h_n, d = HGRN_CHUNK, HGRN_HEADS, HEAD_DIM
    w = h_n * d
    n_c = seq // c_n

    def pspec(cb):
        return pl.BlockSpec((c_n, w), lambda b, c, cb=cb: (b * n_c + c, cb * LANE // w))

    return pl.pallas_call(
        _hgrn_kernel,
        grid=(batch, n_c),
        in_specs=[pspec(CB_HQ), pspec(CB_HF), pspec(CB_HI), pspec(CB_HG),
                  pl.BlockSpec((1, w), lambda b, c: (0, 0)), pl.BlockSpec((1, d), lambda b, c: (0, 0))],
        out_specs=pl.BlockSpec((c_n, w), lambda b, c: (b * n_c + c, 0)),
        out_shape=jax.ShapeDtypeStruct((t, w), BF16),
        scratch_shapes=[pltpu.VMEM((h_n, d, d), F32)],
        compiler_params=_cparams("parallel", "arbitrary"),
        name="hgrn",
    )(p, p, p, p, lb.reshape(1, w), norm_w.reshape(1, d))


def _round_up(x, m):
    return -(-x // m) * m


def _pack_plan():
    offs = [int(v) for v in np.concatenate([[0], np.cumsum(np.array(IN_SIZES))])]
    plan = []
    dst = 0
    for j in (8, 9, 10, 13, 14, 15, 16, 17, 1, 2, 3, 4, 5, 6):
        plan.append((offs[j], IN_SIZES[j], dst))
        dst += IN_SIZES[j]
    plan.append((offs[7], IN_SIZES[7], CB_NGT * LANE))
    plan.append((offs[11], IN_SIZES[11] + IN_SIZES[12], CB_GAB * LANE))
    plan.append((offs[0], IN_SIZES[0], CB_NQ * LANE))
    return tuple(plan)


def _pack_kernel(w_ref, o_ref):
    for src, width, dst in _pack_plan():
        a0 = src // LANE * LANE
        off = src - a0
        span = _round_up(off + width, LANE)
        x = w_ref[:, a0:a0 + span]
        if off:
            x = pltpu.roll(x, span - off, axis=1)
        wout = _round_up(width, LANE)
        x = x[:, :wout]
        if width % LANE:
            x = jnp.where(lax.broadcasted_iota(jnp.int32, x.shape, 1) < width, x, 0.0)
        o_ref[:, dst:dst + wout] = x.astype(o_ref.dtype)


def _pack_w_in(w_in, layer, tr=256):
    _, k, n = w_in.shape
    return pl.pallas_call(
        _pack_kernel,
        grid=(k // tr,),
        in_specs=[pl.BlockSpec((None, tr, _round_up(n, LANE)), lambda i: (layer, i, 0))],
        out_specs=pl.BlockSpec((tr, P_WIDTH), lambda i: (i, 0)),
        out_shape=jax.ShapeDtypeStruct((k, P_WIDTH), BF16),
        compiler_params=_cparams("parallel"),
        name="pack_w_in",
    )(w_in)


def _rope_tables(seq):
    inv = 1.0 / (ROPE_THETA ** (jnp.arange(0, HEAD_DIM, 2, dtype=F32) / HEAD_DIM))
    ang = jnp.arange(seq, dtype=F32)[:, None] * inv[None, :]
    cos, sin = jnp.cos(ang), jnp.sin(ang)
    return jnp.concatenate([cos, cos], axis=1), jnp.concatenate([-sin, sin], axis=1)


def _mixers(p, batch, seq, layer_params):
    (pe_k, pe_v, ck1, ck2, cv1, cv2, conv_w, a_log, dt_bias, gdn_norm, lb, hgrn_norm, cos2, sin2) = layer_params
    cv, kvbf = _nsa_prep(p, cos2, sin2, seq)
    cmp_kv = _nsa_compress(cv, jnp.stack([pe_k, pe_v]), jnp.stack([ck1, cv1]), jnp.stack([ck2, cv2]), batch, seq)
    y_nsa = _nsa_attn(p, cmp_kv, kvbf, cos2, sin2, batch, seq)
    y_gdn = _gdn(p, conv_w, a_log, dt_bias, gdn_norm, batch, seq)
    y_hgrn = _hgrn(p, lb, hgrn_norm, batch, seq)
    return jnp.concatenate([y_nsa, y_gdn, y_hgrn], axis=1)


def kernel(x, ffn1_norm, ffn1_gate, ffn1_up, ffn1_down, mix_norm, w_in, w_out, nsa_pe_k, nsa_pe_v, nsa_ck1, nsa_ck2, nsa_cv1, nsa_cv2, gdn_conv, gdn_a_log, gdn_dt_bias, gdn_norm, hgrn_lb, hgrn_norm, ffn2_norm, ffn2_gate, ffn2_up, ffn2_down, final_norm):
    batch, seq, dm = x.shape
    depth = w_in.shape[0]
    cos2, sin2 = _rope_tables(seq)
    p_lb = jax.nn.softmax(hgrn_lb.astype(F32), axis=0)
    lb_all = jnp.cumsum(p_lb, axis=0) - p_lb[0:1]
    xt = x.reshape(batch * seq, dm)

    def ffn(xt, norm, wg, wu, wd, l):
        h = _rmsnorm(xt, norm[l], BF16)
        act = _ffn_up(h, wg, wu, l)
        return _mm_res(act, wd[l].astype(BF16), xt, 0.5)

    for l in range(depth):
        xt = ffn(xt, ffn1_norm, ffn1_gate, ffn1_up, ffn1_down, l)
        h = _rmsnorm(xt, mix_norm[l], BF16)
        p = _mm(h, _pack_w_in(w_in, l), F32)
        y = _mixers(p, batch, seq, (nsa_pe_k[l], nsa_pe_v[l], nsa_ck1[l], nsa_ck2[l], nsa_cv1[l], nsa_cv2[l],
                                    gdn_conv[l], gdn_a_log[l], gdn_dt_bias[l], gdn_norm[l], lb_all[l], hgrn_norm[l],
                                    cos2, sin2))
        xt = _mm_res(y, w_out[l].astype(BF16), xt, 1.0)
        xt = ffn(xt, ffn2_norm, ffn2_gate, ffn2_up, ffn2_down, l)
    return _rmsnorm(xt, final_norm, F32).reshape(batch, seq, dm)
```

```python
import functools

import jax
import jax.numpy as jnp
import numpy as np
from jax import lax
from jax.experimental import pallas as pl
from jax.experimental.pallas import tpu as pltpu

F32 = jnp.float32
BF16 = jnp.bfloat16
HIGHEST = lax.Precision.HIGHEST

D_MODEL = 2048
HEAD_DIM = 128
ROPE_THETA = 10000.0
EPS = 1e-6
MASK_VALUE = -1e30
FORCE_VALUE = 1e9
MIN_FORGET = 1e-6
NSA_HEADS = 6
NSA_KV_HEADS = 2
NSA_REP = NSA_HEADS // NSA_KV_HEADS
CMP_BLOCK = 32
CMP_STRIDE = 16
SEL_BLOCK = 64
N_SEL = 16
WINDOW = 512
Q_BLOCK = 128
GDN_HEADS = 5
CONV_WIDTH = 4
GDN_CHUNK = 64
HGRN_HEADS = 5
HGRN_CHUNK = 64
D_FF = 5632

NSA_WIDTH = NSA_HEADS * HEAD_DIM
KV_WIDTH = NSA_KV_HEADS * HEAD_DIM
GDN_WIDTH = GDN_HEADS * HEAD_DIM
HGRN_WIDTH = HGRN_HEADS * HEAD_DIM
MIX_WIDTH = NSA_WIDTH + GDN_WIDTH + HGRN_WIDTH
IN_SIZES = (NSA_WIDTH, KV_WIDTH, KV_WIDTH, KV_WIDTH, KV_WIDTH, KV_WIDTH, KV_WIDTH, 3 * NSA_HEADS,
            GDN_WIDTH, GDN_WIDTH, GDN_WIDTH, GDN_HEADS, GDN_HEADS, GDN_WIDTH,
            HGRN_WIDTH, HGRN_WIDTH, HGRN_WIDTH, HGRN_WIDTH)

SEL_TILE = 1024
LANE = 128
VMEM_LIMIT = 48 * 1024 * 1024

CB_GQ = 0
CB_GK = 5
CB_GV = 10
CB_GZ = 15
CB_HQ = 20
CB_HF = 25
CB_HI = 30
CB_HG = 35
CB_NKC = 40
CB_NVC = 42
CB_NKS = 44
CB_NVS = 46
CB_NKW = 48
CB_NVW = 50
CB_NGT = 52
CB_GAB = 53
CB_NQ = 54
P_BLOCKS = 60
P_WIDTH = P_BLOCKS * LANE


def _cparams(*sem):
    return pltpu.CompilerParams(dimension_semantics=sem, vmem_limit_bytes=VMEM_LIMIT)


def _dot(a, b, dims=((1,), (0,)), precision=None):
    return lax.dot_general(a, b, (dims, ((), ())), precision=precision, preferred_element_type=F32)


def _dot_nt(a, b, precision=None):
    return _dot(a, b, ((1,), (1,)), precision)


def _dot_tn(a, b, precision=None):
    return _dot(a, b, ((0,), (0,)), precision)


def _rmsnorm_kernel(x_ref, g_ref, o_ref):
    x = x_ref[...]
    y = x * lax.rsqrt(jnp.mean(x * x, axis=-1, keepdims=True) + EPS)
    o_ref[...] = (y * g_ref[...]).astype(o_ref.dtype)


def _rmsnorm(x, g, out_dtype, tm=512):
    t, d = x.shape
    return pl.pallas_call(
        _rmsnorm_kernel,
        grid=(t // tm,),
        in_specs=[pl.BlockSpec((tm, d), lambda i: (i, 0)), pl.BlockSpec((1, d), lambda i: (0, 0))],
        out_specs=pl.BlockSpec((tm, d), lambda i: (i, 0)),
        out_shape=jax.ShapeDtypeStruct((t, d), out_dtype),
        compiler_params=_cparams("parallel"),
        name="rmsnorm",
    )(x, g.reshape(1, d))


def _ffn_up_kernel(h_ref, wg_ref, wu_ref, o_ref, wg_bf, wu_bf):
    @pl.when(pl.program_id(1) == 0)
    def _():
        wg_bf[...] = wg_ref[...].astype(BF16)
        wu_bf[...] = wu_ref[...].astype(BF16)

    h = h_ref[...]
    a = jnp.dot(h, wg_bf[...], preferred_element_type=F32)
    b = jnp.dot(h, wu_bf[...], preferred_element_type=F32)
    o_ref[...] = (a * jax.nn.sigmoid(a) * b).astype(o_ref.dtype)


def _ffn_up(h, w_gate, w_up, layer, tm=1024, tn=512):
    t, k = h.shape
    n = w_gate.shape[2]
    wspec = pl.BlockSpec((None, k, tn), lambda j, i: (layer, 0, j))
    return pl.pallas_call(
        _ffn_up_kernel,
        grid=(n // tn, t // tm),
        in_specs=[pl.BlockSpec((tm, k), lambda j, i: (i, 0)), wspec, wspec],
        out_specs=pl.BlockSpec((tm, tn), lambda j, i: (i, j)),
        out_shape=jax.ShapeDtypeStruct((t, n), BF16),
        scratch_shapes=[pltpu.VMEM((k, tn), BF16), pltpu.VMEM((k, tn), BF16)],
        compiler_params=_cparams("parallel", "arbitrary"),
        name="ffn_up",
    )(h, w_gate, w_up)


def _mm_kernel(a_ref, b_ref, o_ref):
    o_ref[...] = jnp.dot(a_ref[...], b_ref[...], preferred_element_type=F32).astype(o_ref.dtype)


def _mm(a, b, out_dtype, tm=1024, tn=512):
    t, k = a.shape
    n = b.shape[1]
    return pl.pallas_call(
        _mm_kernel,
        grid=(t // tm, n // tn),
        in_specs=[pl.BlockSpec((tm, k), lambda i, j: (i, 0)), pl.BlockSpec((k, tn), lambda i, j: (0, j))],
        out_specs=pl.BlockSpec((tm, tn), lambda i, j: (i, j)),
        out_shape=jax.ShapeDtypeStruct((t, n), out_dtype),
        compiler_params=_cparams("parallel", "parallel"),
        name="proj_in",
    )(a, b)


def _mm_res_kernel(a_ref, b_ref, r_ref, o_ref, *, scale):
    o_ref[...] = r_ref[...] + scale * jnp.dot(a_ref[...], b_ref[...], preferred_element_type=F32)


def _mm_res(a, b, res, scale, tm=1024, tn=512):
    t, kdim = a.shape
    n = b.shape[1]
    return pl.pallas_call(
        functools.partial(_mm_res_kernel, scale=scale),
        grid=(t // tm, n // tn),
        in_specs=[pl.BlockSpec((tm, kdim), lambda i, j: (i, 0)),
                  pl.BlockSpec((kdim, tn), lambda i, j: (0, j)),
                  pl.BlockSpec((tm, tn), lambda i, j: (i, j))],
        out_specs=pl.BlockSpec((tm, tn), lambda i, j: (i, j)),
        out_shape=jax.ShapeDtypeStruct((t, n), F32),
        compiler_params=_cparams("parallel", "parallel"),
        name="mm_residual",
    )(a, b, res)


def _rope(x, cos2, sin2):
    return x * cos2 + pltpu.roll(x, HEAD_DIM // 2, axis=1) * sin2


def _nsa_prep_kernel(kc_ref, vc_ref, ks_ref, vs_ref, kw_ref, vw_ref, cos_ref, sin_ref, cv_ref, kv_ref):
    cos2 = cos_ref[...]
    sin2 = sin_ref[...]
    for g in range(NSA_KV_HEADS):
        sl = slice(g * HEAD_DIM, (g + 1) * HEAD_DIM)
        cv_ref[g] = _rope(kc_ref[:, sl], cos2, sin2)
        cv_ref[NSA_KV_HEADS + g] = vc_ref[:, sl]
        for j, (ref, roped) in enumerate(((ks_ref, True), (vs_ref, False), (kw_ref, True), (vw_ref, False))):
            x = ref[:, sl]
            if roped:
                x = _rope(x, cos2, sin2)
            c0 = (j * NSA_KV_HEADS + g) * HEAD_DIM
            kv_ref[:, c0:c0 + HEAD_DIM] = x.astype(BF16)


def _nsa_prep(p, cos2, sin2, seq, tile=512):
    t = p.shape[0]
    per_seq = seq // tile
    kvw = KV_WIDTH

    def pspec(cb):
        return pl.BlockSpec((tile, kvw), lambda i, cb=cb: (i, cb * LANE // kvw))

    tab = pl.BlockSpec((tile, HEAD_DIM), lambda i: (i % per_seq, 0))
    return pl.pallas_call(
        _nsa_prep_kernel,
        grid=(t // tile,),
        in_specs=[pspec(CB_NKC), pspec(CB_NVC), pspec(CB_NKS), pspec(CB_NVS), pspec(CB_NKW), pspec(CB_NVW), tab, tab],
        out_specs=[pl.BlockSpec((2 * NSA_KV_HEADS, tile, HEAD_DIM), lambda i: (0, i, 0)),
                   pl.BlockSpec((tile, 4 * kvw), lambda i: (i, 0))],
        out_shape=[jax.ShapeDtypeStruct((2 * NSA_KV_HEADS, t, HEAD_DIM), F32),
                   jax.ShapeDtypeStruct((t, 4 * kvw), BF16)],
        compiler_params=_cparams("parallel"),
        name="nsa_prep",
    )(p, p, p, p, p, p, cos2, sin2)


def _nsa_compress_kernel(t_ref, pe_ref, w1_ref, w2_ref, o_ref):
    x = t_ref[...]
    half = CMP_STRIDE * HEAD_DIM
    nr = x.shape[0]
    u = _dot(x + pe_ref[0:1, :], w1_ref[0:half, :], precision=HIGHEST)
    v = _dot(x + pe_ref[1:2, :], w1_ref[half:2 * half, :], precision=HIGHEST)
    pre = u + pltpu.roll(v, nr - 1, axis=0)
    o_ref[...] = _dot(jax.nn.gelu(pre), w2_ref[...], precision=HIGHEST)


def _nsa_compress(cv, pe, w1, w2, batch, seq):
    nr = seq // CMP_STRIDE
    half = CMP_STRIDE * HEAD_DIM
    t2 = cv.reshape(2 * NSA_KV_HEADS, batch, nr, half)
    return pl.pallas_call(
        _nsa_compress_kernel,
        grid=(2 * NSA_KV_HEADS, batch),
        in_specs=[pl.BlockSpec((None, None, nr, half), lambda c, b: (c, b, 0, 0)),
                  pl.BlockSpec((None, 2, half), lambda c, b: (c // NSA_KV_HEADS, 0, 0)),
                  pl.BlockSpec((None, 2 * half, HEAD_DIM), lambda c, b: (c // NSA_KV_HEADS, 0, 0)),
                  pl.BlockSpec((None, HEAD_DIM, HEAD_DIM), lambda c, b: (c // NSA_KV_HEADS, 0, 0))],
        out_specs=pl.BlockSpec((None, None, nr, HEAD_DIM), lambda c, b: (c, b, 0, 0)),
        out_shape=jax.ShapeDtypeStruct((2 * NSA_KV_HEADS, batch, nr, HEAD_DIM), F32),
        compiler_params=_cparams("parallel", "parallel"),
        name="nsa_compress",
    )(t2, pe.reshape(2, 2, half), w1, w2)


def _masked_softmax(s, mask):
    s = jnp.where(mask, s, MASK_VALUE)
    m = jnp.max(s, axis=-1, keepdims=True)
    e = jnp.where(mask, jnp.exp(s - m), 0.0)
    return e / jnp.maximum(jnp.sum(e, axis=-1, keepdims=True), 1e-30)


def _nsa_attn_kernel(q_ref, gt_ref, cos_ref, sin_ref, kc_ref, vc_ref, ks_ref, vs_ref, kw_ref, vw_ref,
                     o_ref, m_ref, l_ref, acc_ref, *, seq):
    g = pl.program_id(1)
    i = pl.program_id(2)
    qn, r_n, d = Q_BLOCK, NSA_REP, HEAD_DIM
    rows = r_n * qn
    n_blk = seq // SEL_BLOCK
    n_sel = min(N_SEL, n_blk)
    ncr = seq // CMP_STRIDE

    cos2 = cos_ref[...]
    sin2 = sin_ref[...]
    qb = q_ref[...]
    q = jnp.concatenate([_rope(qb[:, r * d:(r + 1) * d], cos2, sin2) * (HEAD_DIM ** -0.5) for r in range(r_n)],
                        axis=0)
    q_bf = q.astype(BF16)
    t0 = i * qn
    t_col = t0 + lax.broadcasted_iota(jnp.int32, (qn, 1), 0)
    t_rows = jnp.concatenate([t_col] * r_n, axis=0)

    kc = kc_ref[...]
    vc = vc_ref[...]
    s_c = _dot3(_split_bf16(q), _split_bf16(kc), ((1,), (1,)))
    cmp_end = lax.broadcasted_iota(jnp.int32, (1, ncr), 1) * CMP_STRIDE + (CMP_BLOCK - 1)
    p_c = _masked_softmax(s_c, cmp_end <= t_rows)
    o_c = _dot(p_c.astype(BF16), vc.astype(BF16))

    pc_sum = p_c[0:qn]
    for r in range(1, r_n):
        pc_sum = pc_sum + p_c[r * qn:(r + 1) * qn]
    m_i = lax.broadcasted_iota(jnp.int32, (n_blk, ncr), 0)
    c_i = lax.broadcasted_iota(jnp.int32, (n_blk, ncr), 1)
    ov = (jnp.minimum(c_i * CMP_STRIDE + CMP_BLOCK, m_i * SEL_BLOCK + SEL_BLOCK)
          - jnp.maximum(c_i * CMP_STRIDE, m_i * SEL_BLOCK))
    ov_t = jnp.where(c_i < ncr - 1, jnp.maximum(ov, 0).astype(F32) / CMP_BLOCK, 0.0)
    ov_bf = ov_t.astype(BF16)
    pc_hi, pc_lo = _split_bf16(pc_sum)
    imp = _dot_nt(ov_bf, pc_hi) + _dot_nt(ov_bf, pc_lo)
    blk = lax.broadcasted_iota(jnp.int32, (n_blk, qn), 0)
    t_row = t0 + lax.broadcasted_iota(jnp.int32, (n_blk, qn), 1)
    cur = t_row // SEL_BLOCK
    forced = (blk == 0) | (blk == cur) | (blk == cur - 1)
    future = blk * SEL_BLOCK > t_row
    imp = jnp.where(forced, FORCE_VALUE, jnp.where(future, -FORCE_VALUE, imp))
    rank = jnp.zeros((n_blk, qn), F32)
    for j in range(n_blk):
        row = imp[j:j + 1, :]
        beats = (row > imp) | ((row == imp) & (blk > j))
        rank = rank + jnp.where(beats, 1.0, 0.0)
    sel_t = jnp.where(rank < n_sel, 1.0, 0.0)
    sel = sel_t.T.astype(BF16)

    m_ref[...] = jnp.full((rows, 1), MASK_VALUE, F32)
    l_ref[...] = jnp.zeros((rows, 1), F32)
    acc_ref[...] = jnp.zeros((rows, d), F32)
    n_tiles = (t0 + qn + SEL_TILE - 1) // SEL_TILE

    def sel_step(kt, carry):
        k0 = pl.multiple_of(kt * SEL_TILE, SEL_TILE)
        s = _dot_nt(q_bf, ks_ref[pl.ds(k0, SEL_TILE), :])
        kpos = k0 + lax.broadcasted_iota(jnp.int32, (1, SEL_TILE), 1)
        e_blk = lax.broadcasted_iota(jnp.int32, (n_blk, SEL_TILE), 0)
        e_key = k0 + lax.broadcasted_iota(jnp.int32, (n_blk, SEL_TILE), 1)
        expand = jnp.where(e_blk == e_key // SEL_BLOCK, 1.0, 0.0).astype(BF16)
        chosen = _dot(sel, expand)
        ok_q = (chosen > 0.5) & (kpos <= t_col)
        ok = jnp.concatenate([ok_q] * r_n, axis=0)
        s = jnp.where(ok, s, MASK_VALUE)
        m_old = m_ref[...]
        m_new = jnp.maximum(m_old, jnp.max(s, axis=-1, keepdims=True))
        alpha = jnp.exp(m_old - m_new)
        p = jnp.exp(s - m_new)
        l_ref[...] = alpha * l_ref[...] + jnp.sum(p, axis=-1, keepdims=True)
        acc_ref[...] = alpha * acc_ref[...] + _dot(p.astype(BF16), vs_ref[pl.ds(k0, SEL_TILE), :])
        m_ref[...] = m_new
        return carry

    lax.fori_loop(0, n_tiles, sel_step, 0)
    o_s = acc_ref[...] / jnp.maximum(l_ref[...], 1e-30)

    wlen = WINDOW + qn
    w0 = pl.multiple_of(jnp.maximum(t0 - WINDOW, 0), qn)
    kw = kw_ref[pl.ds(w0, wlen), :]
    vw = vw_ref[pl.ds(w0, wlen), :]
    s_w = _dot_nt(q_bf, kw)
    wpos = w0 + lax.broadcasted_iota(jnp.int32, (1, wlen), 1)
    p_w = _masked_softmax(s_w, (wpos <= t_rows) & (wpos > t_rows - WINDOW))
    o_w = _dot(p_w.astype(BF16), vw)

    gates = jax.nn.sigmoid(gt_ref[...])
    outs = []
    for r in range(r_n):
        sl = slice(r * qn, (r + 1) * qn)
        o_r = 0.0
        for c, o_b in enumerate((o_c, o_s, o_w)):
            lane = r * 3 + c
            gcol = jnp.where(g == 0, gates[:, lane:lane + 1], gates[:, 3 * r_n + lane:3 * r_n + lane + 1])
            o_r = o_r + gcol * o_b[sl]
        outs.append(o_r)
    o_ref[...] = jnp.concatenate(outs, axis=1).astype(o_ref.dtype)


def _nsa_attn(p, cmp_kv, kvbf, cos2, sin2, batch, seq):
    t = p.shape[0]
    n_qb = seq // Q_BLOCK
    ncr = seq // CMP_STRIDE
    gw = NSA_REP * HEAD_DIM
    rows = NSA_REP * Q_BLOCK

    def kv_spec(j):
        return pl.BlockSpec((seq, HEAD_DIM), lambda b, g, i, j=j: (b, j * NSA_KV_HEADS + g))

    return pl.pallas_call(
        functools.partial(_nsa_attn_kernel, seq=seq),
        grid=(batch, NSA_KV_HEADS, n_qb),
        in_specs=[pl.BlockSpec((Q_BLOCK, gw), lambda b, g, i: (b * n_qb + i, CB_NQ // NSA_REP + g)),
                  pl.BlockSpec((Q_BLOCK, LANE), lambda b, g, i: (b * n_qb + i, CB_NGT)),
                  pl.BlockSpec((Q_BLOCK, HEAD_DIM), lambda b, g, i: (i, 0)),
                  pl.BlockSpec((Q_BLOCK, HEAD_DIM), lambda b, g, i: (i, 0)),
                  pl.BlockSpec((None, None, ncr, HEAD_DIM), lambda b, g, i: (g, b, 0, 0)),
                  pl.BlockSpec((None, None, ncr, HEAD_DIM), lambda b, g, i: (NSA_KV_HEADS + g, b, 0, 0)),
                  kv_spec(0), kv_spec(1), kv_spec(2), kv_spec(3)],
        out_specs=pl.BlockSpec((Q_BLOCK, gw), lambda b, g, i: (b * n_qb + i, g)),
        out_shape=jax.ShapeDtypeStruct((t, NSA_WIDTH), BF16),
        scratch_shapes=[pltpu.VMEM((rows, 1), F32), pltpu.VMEM((rows, 1), F32), pltpu.VMEM((rows, HEAD_DIM), F32)],
        compiler_params=_cparams("parallel", "parallel", "arbitrary"),
        name="nsa_attn",
    )(p, p, cos2, sin2, cmp_kv, cmp_kv, kvbf, kvbf, kvbf, kvbf)


def _tri(n, lower_inclusive=True):
    r = lax.broadcasted_iota(jnp.int32, (n, n), 0)
    c = lax.broadcasted_iota(jnp.int32, (n, n), 1)
    return r, c


def _softplus(x):
    return jnp.maximum(x, 0.0) + jnp.log1p(jnp.exp(-jnp.abs(x)))


def _gated_rmsnorm(o, norm_w, gate):
    y = o * lax.rsqrt(jnp.mean(o * o, axis=-1, keepdims=True) + EPS)
    return (y * norm_w) * (gate * jax.nn.sigmoid(gate))


GDN_TILE = 256
GDN_SCAN_TILE = 128


def _split_bf16(x):
    hi = x.astype(BF16)
    return hi, (x - hi.astype(F32)).astype(BF16)


def _dot3(a, b, dims=((1,), (0,))):
    return _dot(a[0], b[0], dims) + (_dot(a[0], b[1], dims) + _dot(a[1], b[0], dims))


def _chunk_cumsum(x, chunk):
    pos = lax.broadcasted_iota(jnp.int32, (x.shape[0], 1), 0) % chunk
    s = 1
    while s < chunk:
        x = x + jnp.where(pos >= s, pltpu.roll(x, s, axis=0), 0.0)
        s *= 2
    return x


def _gdn_prep_kernel(q_ref, k_ref, v_ref, ab_ref, wconv_ref, alog_ref, dtb_ref,
                     u_ref, w_ref, qd_ref, att_ref, kdt_ref, dl_ref, tail_ref):
    c_n, d, h_n, tb = GDN_CHUNK, HEAD_DIM, GDN_HEADS, GDN_TILE
    nch = tb // c_n
    w = h_n * d

    @pl.when(pl.program_id(1) == 0)
    def _():
        tail_ref[...] = jnp.zeros_like(tail_ref)

    x = jnp.concatenate([q_ref[...], k_ref[...], v_ref[...]], axis=1)
    xcat = jnp.concatenate([tail_ref[...], x], axis=0)
    wc = wconv_ref[...]
    conv = None
    for j in range(CONV_WIDTH):
        shift = CONV_WIDTH - 1 - j
        xs = xcat if shift == 0 else pltpu.roll(xcat, shift, axis=0)
        term = xs[8:8 + tb] * wc[j:j + 1, :]
        conv = term if conv is None else conv + term
    tail_ref[...] = x[tb - 8:tb]
    act = conv * jax.nn.sigmoid(conv)

    ab = ab_ref[...]
    g_all = -jnp.exp(alog_ref[...]) * _softplus(ab + dtb_ref[...])
    beta_all = jax.nn.sigmoid(ab)
    gc_all = _chunk_cumsum(g_all, c_n)
    gl_all = jnp.concatenate([jnp.broadcast_to(gc_all[(c + 1) * c_n - 1:(c + 1) * c_n], (c_n, LANE))
                              for c in range(nch)], axis=0)
    gct_all = gc_all.T
    dl_ref[...] = jnp.exp(gl_all)
    r_i, c_i = _tri(tb)
    same = (r_i // c_n) == (c_i // c_n)
    causal = same & (r_i >= c_i)
    strict = same & (r_i > c_i)
    eye = jnp.where(r_i == c_i, 1.0, 0.0)
    lane_i = lax.broadcasted_iota(jnp.int32, (d, LANE), 1)

    for h in range(h_n):
        sl = slice(h * d, (h + 1) * d)
        xq = act[:, h * d:(h + 1) * d]
        xk = act[:, w + h * d:w + (h + 1) * d]
        v = act[:, 2 * w + h * d:2 * w + (h + 1) * d]
        q = xq * lax.rsqrt(jnp.sum(xq * xq, axis=-1, keepdims=True) + EPS) * (HEAD_DIM ** -0.5)
        k = xk * lax.rsqrt(jnp.sum(xk * xk, axis=-1, keepdims=True) + EPS)
        beta = beta_all[:, h_n + h:h_n + h + 1]
        gc = gc_all[:, h:h + 1]
        gc_row = gct_all[h:h + 1, :]
        decay = jnp.where(causal, jnp.exp(jnp.where(causal, gc - gc_row, 0.0)), 0.0)
        kb = k * beta
        qk = _dot_nt(jnp.concatenate([q, kb], axis=0).astype(BF16), k.astype(BF16))
        attn = qk[:tb] * decay
        a_mat = jnp.where(strict, qk[tb:] * decay, 0.0)
        egc = jnp.exp(gc)
        rhs = jnp.concatenate([v * beta, kb * egc], axis=1)
        pw = -a_mat
        t_inv = eye + pw
        pw_s = _split_bf16(pw)
        pw = _dot3(pw_s, pw_s)
        pw_b = pw.astype(BF16)
        t_inv = t_inv + _dot3(_split_bf16(t_inv), _split_bf16(pw))
        span = 4
        while span < c_n:
            pw = _dot(pw_b, pw_b)
            pw_b = pw.astype(BF16)
            t_inv = t_inv + _dot(t_inv.astype(BF16), pw_b)
            span *= 2
        sol = _dot3(_split_bf16(t_inv), _split_bf16(rhs))
        u_ref[:, sl] = sol[:, :d]
        w_ref[:, sl] = sol[:, d:].astype(BF16)
        qd_ref[:, sl] = (q * egc).astype(BF16)
        att_ref[:, sl] = jnp.concatenate([attn[j * LANE:(j + 1) * LANE, j * LANE:(j + 1) * LANE]
                                          for j in range(tb // LANE)], axis=0).astype(BF16)
        kdt = (k * jnp.exp(gl_all[:, h:h + 1] - gc)).T
        for c in range(nch):
            blk = kdt[:, (c // 2) * LANE:(c // 2 + 1) * LANE]
            if c % 2:
                blk = pltpu.roll(blk, c_n, axis=1)
            kdt_ref[c, h] = jnp.where(lane_i < c_n, blk, 0.0).astype(BF16)


def _gdn_scan_kernel(u_ref, w_ref, qd_ref, att_ref, kdt_ref, dl_ref, z_ref, nw_ref, o_ref, state_ref):
    c_n, d, h_n = GDN_CHUNK, HEAD_DIM, GDN_HEADS
    n_b = u_ref.shape[0]

    @pl.when(pl.program_id(0) == 0)
    def _():
        state_ref[...] = jnp.zeros_like(state_ref)

    nw = nw_ref[...]
    for c in range(GDN_SCAN_TILE // c_n):
        rs = slice(c * c_n, (c + 1) * c_n)
        for b in range(n_b):
            for h in range(h_n):
                sl = slice(h * d, (h + 1) * d)
                state = state_ref[b * h_n + h]
                r = _dot(jnp.concatenate([w_ref[b, rs, sl], qd_ref[b, rs, sl]], axis=0), state.astype(BF16))
                v_new = (u_ref[b, rs, sl] - r[:c_n]).astype(BF16)
                vv = jnp.concatenate([v_new, v_new], axis=0)
                o = r[c_n:] + _dot(att_ref[b, rs, sl], vv)
                state_ref[b * h_n + h] = (state * dl_ref[b, c * c_n:c * c_n + 1, h:h + 1]
                                          + _dot(kdt_ref[b, c, h], vv))
                o_ref[b, rs, sl] = _gated_rmsnorm(o, nw, z_ref[b, rs, sl]).astype(o_ref.dtype)


def _gdn(p, w_conv, a_log, dt_bias, norm_w, batch, seq):
    t = p.shape[0]
    c_n, h_n, d = GDN_CHUNK, GDN_HEADS, HEAD_DIM
    w = h_n * d
    tb, ts = GDN_TILE, GDN_SCAN_TILE
    pad = jnp.zeros((LANE - h_n,), F32)
    alog = jnp.concatenate([a_log, pad]).reshape(1, LANE)
    dtb = jnp.concatenate([dt_bias, pad]).reshape(1, LANE)

    def whole(shape):
        return pl.BlockSpec(shape, lambda b, c: (0,) * len(shape))

    n_t = seq // tb

    def pspec(cb, width):
        return pl.BlockSpec((tb, width), lambda b, i, cb=cb, width=width: (b * n_t + i, cb * LANE // width))

    row_w = pl.BlockSpec((tb, w), lambda b, i: (b * n_t + i, 0))
    u, wm, qd, att, kdt, dl = pl.pallas_call(
        _gdn_prep_kernel,
        grid=(batch, n_t),
        in_specs=[pspec(CB_GQ, w), pspec(CB_GK, w), pspec(CB_GV, w), pspec(CB_GAB, LANE),
                  whole((CONV_WIDTH, 3 * w)), whole((1, LANE)), whole((1, LANE))],
        out_specs=[row_w, row_w, row_w, row_w,
                   pl.BlockSpec((tb // c_n, h_n, d, LANE), lambda b, i: (b * n_t + i, 0, 0, 0)),
                   pl.BlockSpec((tb, LANE), lambda b, i: (b * n_t + i, 0))],
        out_shape=[jax.ShapeDtypeStruct((t, w), F32), jax.ShapeDtypeStruct((t, w), BF16),
                   jax.ShapeDtypeStruct((t, w), BF16), jax.ShapeDtypeStruct((t, w), BF16),
                   jax.ShapeDtypeStruct((t // c_n, h_n, d, LANE), BF16), jax.ShapeDtypeStruct((t, LANE), F32)],
        scratch_shapes=[pltpu.VMEM((8, 3 * w), F32)],
        compiler_params=_cparams("parallel", "arbitrary"),
        name="gdn_prep",
    )(p, p, p, p, w_conv, alog, dtb)

    def per_batch(a):
        return a.reshape((batch, a.shape[0] // batch) + a.shape[1:])

    row_s = pl.BlockSpec((batch, ts, w), lambda i: (0, i, 0))
    y = pl.pallas_call(
        _gdn_scan_kernel,
        grid=(seq // ts,),
        in_specs=[row_s, row_s, row_s, row_s,
                  pl.BlockSpec((batch, ts // c_n, h_n, d, LANE), lambda i: (0, i, 0, 0, 0)),
                  pl.BlockSpec((batch, ts, LANE), lambda i: (0, i, 0)),
                  pl.BlockSpec((batch, ts, w), lambda i: (0, i, CB_GZ * LANE // w)),
                  pl.BlockSpec((1, d), lambda i: (0, 0))],
        out_specs=row_s,
        out_shape=jax.ShapeDtypeStruct((batch, seq, w), BF16),
        scratch_shapes=[pltpu.VMEM((batch * h_n, d, d), F32)],
        compiler_params=_cparams("arbitrary"),
        name="gdn_scan",
    )(per_batch(u), per_batch(wm), per_batch(qd), per_batch(att), per_batch(kdt), per_batch(dl), per_batch(p),
      norm_w.reshape(1, d))
    return y.reshape(t, w)


HGRN_SUB = 16


def _hgrn_kernel(q_ref, f_ref, i_ref, g_ref, lb_ref, nw_ref, o_ref, state_ref):
    c_n, d, h_n, sb = HGRN_CHUNK, HEAD_DIM, HGRN_HEADS, HGRN_SUB
    ci = pl.program_id(1)

    @pl.when(ci == 0)
    def _():
        state_ref[...] = jnp.zeros_like(state_ref)

    lane_i = lax.broadcasted_iota(jnp.int32, (sb, c_n), 1)
    nw = nw_ref[...]
    outs = []
    for h in range(h_n):
        sl = slice(h * d, (h + 1) * d)
        lb = lb_ref[:, sl]
        f_gate = lb + (1.0 - lb) * jax.nn.sigmoid(f_ref[:, sl])
        log_f = jnp.log(jnp.maximum(f_gate, MIN_FORGET))
        k = 1.0 - f_gate
        q = q_ref[:, sl]
        v = i_ref[:, sl]
        bcum = _chunk_cumsum(log_f, c_n)
        a_rows = []
        for sbi in range(c_n // sb):
            r0 = sbi * sb
            q_s = q[r0:r0 + sb]
            b_s = bcum[r0:r0 + sb]
            a_blk = jnp.zeros((sb, c_n), F32)
            if sbi > 0:
                b_ref0 = bcum[r0:r0 + 1]
                q_t = q_s * jnp.exp(b_s - b_ref0)
                k_t = k * jnp.exp(jnp.minimum(b_ref0 - bcum, 0.0))
                a_blk = jnp.where(lane_i < r0, _dot3(_split_bf16(q_t), _split_bf16(k_t), ((1,), (1,))), 0.0)
            for jj in range(sb):
                j = r0 + jj
                ok = lax.broadcasted_iota(jnp.int32, (sb, 1), 0) >= jj
                e = jnp.where(ok, jnp.exp(jnp.where(ok, b_s - bcum[j:j + 1], 0.0)), 0.0)
                col = jnp.sum(q_s * k[j:j + 1] * e, axis=-1, keepdims=True)
                a_blk = jnp.where(lane_i == j, col, a_blk)
            a_rows.append(a_blk)
        a_mat = jnp.concatenate(a_rows, axis=0)
        state_t = state_ref[h]
        v_bf = v.astype(BF16)
        o = _dot_nt((q * jnp.exp(bcum)).astype(BF16), state_t.astype(BF16)) + _dot(a_mat.astype(BF16), v_bf)
        b_last = bcum[c_n - 1:c_n]
        k_dec = k * jnp.exp(b_last - bcum)
        state_ref[h] = state_t * jnp.exp(b_last) + _dot(v.T.astype(BF16), k_dec.astype(BF16))
        outs.append(_gated_rmsnorm(o, nw, g_ref[:, sl]))
    o_ref[...] = jnp.concatenate(outs, axis=1).astype(o_ref.dtype)


def _hgrn(p, lb, norm_w, batch, seq):
    t = p.shape[0]
    c_n, h_n, d = HGRN_CHUNK, HGRN_HEADS, HEAD_DIM
    w = h_n * d
    n_c = seq // c_n

    def pspec(cb):
        return pl.BlockSpec((c_n, w), lambda b, c, cb=cb: (b * n_c + c, cb * LANE // w))

    return pl.pallas_call(
        _hgrn_kernel,
        grid=(batch, n_c),
        in_specs=[pspec(CB_HQ), pspec(CB_HF), pspec(CB_HI), pspec(CB_HG),
                  pl.BlockSpec((1, w), lambda b, c: (0, 0)), pl.BlockSpec((1, d), lambda b, c: (0, 0))],
        out_specs=pl.BlockSpec((c_n, w), lambda b, c: (b * n_c + c, 0)),
        out_shape=jax.ShapeDtypeStruct((t, w), BF16),
        scratch_shapes=[pltpu.VMEM((h_n, d, d), F32)],
        compiler_params=_cparams("parallel", "arbitrary"),
        name="hgrn",
    )(p, p, p, p, lb.reshape(1, w), norm_w.reshape(1, d))


def _round_up(x, m):
    return -(-x // m) * m


def _pack_plan():
    offs = [int(v) for v in np.concatenate([[0], np.cumsum(np.array(IN_SIZES))])]
    plan = []
    dst = 0
    for j in (8, 9, 10, 13, 14, 15, 16, 17, 1, 2, 3, 4, 5, 6):
        plan.append((offs[j], IN_SIZES[j], dst))
        dst += IN_SIZES[j]
    plan.append((offs[7], IN_SIZES[7], CB_NGT * LANE))
    plan.append((offs[11], IN_SIZES[11] + IN_SIZES[12], CB_GAB * LANE))
    plan.append((offs[0], IN_SIZES[0], CB_NQ * LANE))
    return tuple(plan)


def _pack_kernel(w_ref, o_ref):
    for src, width, dst in _pack_plan():
        a0 = src // LANE * LANE
        off = src - a0
        span = _round_up(off + width, LANE)
        x = w_ref[:, a0:a0 + span]
        if off:
            x = pltpu.roll(x, span - off, axis=1)
        wout = _round_up(width, LANE)
        x = x[:, :wout]
        if width % LANE:
            x = jnp.where(lax.broadcasted_iota(jnp.int32, x.shape, 1) < width, x, 0.0)
        o_ref[:, dst:dst + wout] = x.astype(o_ref.dtype)


def _pack_w_in(w_in, layer, tr=256):
    _, k, n = w_in.shape
    return pl.pallas_call(
        _pack_kernel,
        grid=(k // tr,),
        in_specs=[pl.BlockSpec((None, tr, _round_up(n, LANE)), lambda i: (layer, i, 0))],
        out_specs=pl.BlockSpec((tr, P_WIDTH), lambda i: (i, 0)),
        out_shape=jax.ShapeDtypeStruct((k, P_WIDTH), BF16),
        compiler_params=_cparams("parallel"),
        name="pack_w_in",
    )(w_in)


def _rope_tables(seq):
    inv = 1.0 / (ROPE_THETA ** (jnp.arange(0, HEAD_DIM, 2, dtype=F32) / HEAD_DIM))
    ang = jnp.arange(seq, dtype=F32)[:, None] * inv[None, :]
    cos, sin = jnp.cos(ang), jnp.sin(ang)
    return jnp.concatenate([cos, cos], axis=1), jnp.concatenate([-sin, sin], axis=1)


def _mixers(p, batch, seq, layer_params):
    (pe_k, pe_v, ck1, ck2, cv1, cv2, conv_w, a_log, dt_bias, gdn_norm, lb, hgrn_norm, cos2, sin2) = layer_params
    cv, kvbf = _nsa_prep(p, cos2, sin2, seq)
    cmp_kv = _nsa_compress(cv, jnp.stack([pe_k, pe_v]), jnp.stack([ck1, cv1]), jnp.stack([ck2, cv2]), batch, seq)
    y_nsa = _nsa_attn(p, cmp_kv, kvbf, cos2, sin2, batch, seq)
    y_gdn = _gdn(p, conv_w, a_log, dt_bias, gdn_norm, batch, seq)
    y_hgrn = _hgrn(p, lb, hgrn_norm, batch, seq)
    return jnp.concatenate([y_nsa, y_gdn, y_hgrn], axis=1)


def kernel(x, ffn1_norm, ffn1_gate, ffn1_up, ffn1_down, mix_norm, w_in, w_out, nsa_pe_k, nsa_pe_v, nsa_ck1, nsa_ck2, nsa_cv1, nsa_cv2, gdn_conv, gdn_a_log, gdn_dt_bias, gdn_norm, hgrn_lb, hgrn_norm, ffn2_norm, ffn2_gate, ffn2_up, ffn2_down, final_norm):
    batch, seq, dm = x.shape
    depth = w_in.shape[0]
    cos2, sin2 = _rope_tables(seq)
    p_lb = jax.nn.softmax(hgrn_lb.astype(F32), axis=0)
    lb_all = jnp.cumsum(p_lb, axis=0) - p_lb[0:1]
    xt = x.reshape(batch * seq, dm)

    def ffn(xt, norm, wg, wu, wd, l):
        h = _rmsnorm(xt, norm[l], BF16)
        act = _ffn_up(h, wg, wu, l)
        return _mm_res(act, wd[l].astype(BF16), xt, 0.5)

    for l in range(depth):
        xt = ffn(xt, ffn1_norm, ffn1_gate, ffn1_up, ffn1_down, l)
        h = _rmsnorm(xt, mix_norm[l], BF16)
        p = _mm(h, _pack_w_in(w_in, l), F32)
        y = _mixers(p, batch, seq, (nsa_pe_k[l], nsa_pe_v[l], nsa_ck1[l], nsa_ck2[l], nsa_cv1[l], nsa_cv2[l],
                                    gdn_conv[l], gdn_a_log[l], gdn_dt_bias[l], gdn_norm[l], lb_all[l], hgrn_norm[l],
                                    cos2, sin2))
        xt = _mm_res(y, w_out[l].astype(BF16), xt, 1.0)
        xt = ffn(xt, ffn2_norm, ffn2_gate, ffn2_up, ffn2_down, l)
    return _rmsnorm(xt, final_norm, F32).reshape(batch, seq, dm)
```

```python
import functools

import jax
import jax.numpy as jnp
import numpy as np
from jax import lax
from jax.experimental import pallas as pl
from jax.experimental.pallas import tpu as pltpu

F32 = jnp.float32
BF16 = jnp.bfloat16
HIGHEST = lax.Precision.HIGHEST

D_MODEL = 2048
HEAD_DIM = 128
ROPE_THETA = 10000.0
EPS = 1e-6
MASK_VALUE = -1e30
FORCE_VALUE = 1e9
MIN_FORGET = 1e-6
NSA_HEADS = 6
NSA_KV_HEADS = 2
NSA_REP = NSA_HEADS // NSA_KV_HEADS
CMP_BLOCK = 32
CMP_STRIDE = 16
SEL_BLOCK = 64
N_SEL = 16
WINDOW = 512
Q_BLOCK = 256
GDN_HEADS = 5
CONV_WIDTH = 4
GDN_CHUNK = 64
HGRN_HEADS = 5
HGRN_CHUNK = 64
D_FF = 5632

NSA_WIDTH = NSA_HEADS * HEAD_DIM
KV_WIDTH = NSA_KV_HEADS * HEAD_DIM
GDN_WIDTH = GDN_HEADS * HEAD_DIM
HGRN_WIDTH = HGRN_HEADS * HEAD_DIM
MIX_WIDTH = NSA_WIDTH + GDN_WIDTH + HGRN_WIDTH
IN_SIZES = (NSA_WIDTH, KV_WIDTH, KV_WIDTH, KV_WIDTH, KV_WIDTH, KV_WIDTH, KV_WIDTH, 3 * NSA_HEADS,
            GDN_WIDTH, GDN_WIDTH, GDN_WIDTH, GDN_HEADS, GDN_HEADS, GDN_WIDTH,
            HGRN_WIDTH, HGRN_WIDTH, HGRN_WIDTH, HGRN_WIDTH)

SEL_TILE = 1024
LANE = 128
VMEM_LIMIT = 48 * 1024 * 1024

CB_GQ = 0
CB_GK = 5
CB_GV = 10
CB_GZ = 15
CB_HQ = 20
CB_HF = 25
CB_HI = 30
CB_HG = 35
CB_NKC = 40
CB_NVC = 42
CB_NKS = 44
CB_NVS = 46
CB_NKW = 48
CB_NVW = 50
CB_NGT = 52
CB_GAB = 53
CB_NQ = 54
P_BLOCKS = 60
P_WIDTH = P_BLOCKS * LANE


def _cparams(*sem):
    return pltpu.CompilerParams(dimension_semantics=sem, vmem_limit_bytes=VMEM_LIMIT)


def _dot(a, b, dims=((1,), (0,)), precision=None):
    return lax.dot_general(a, b, (dims, ((), ())), precision=precision, preferred_element_type=F32)


def _dot_nt(a, b, precision=None):
    return _dot(a, b, ((1,), (1,)), precision)


def _dot_tn(a, b, precision=None):
    return _dot(a, b, ((0,), (0,)), precision)


def _rmsnorm_kernel(x_ref, g_ref, o_ref):
    x = x_ref[...]
    y = x * lax.rsqrt(jnp.mean(x * x, axis=-1, keepdims=True) + EPS)
    o_ref[...] = (y * g_ref[...]).astype(o_ref.dtype)


def _rmsnorm(x, g, out_dtype, tm=512):
    t, d = x.shape
    return pl.pallas_call(
        _rmsnorm_kernel,
        grid=(t // tm,),
        in_specs=[pl.BlockSpec((tm, d), lambda i: (i, 0)), pl.BlockSpec((1, d), lambda i: (0, 0))],
        out_specs=pl.BlockSpec((tm, d), lambda i: (i, 0)),
        out_shape=jax.ShapeDtypeStruct((t, d), out_dtype),
        compiler_params=_cparams("parallel"),
        name="rmsnorm",
    )(x, g.reshape(1, d))


def _ffn_up_kernel(h_ref, wg_ref, wu_ref, o_ref, wg_bf, wu_bf):
    @pl.when(pl.program_id(1) == 0)
    def _():
        wg_bf[...] = wg_ref[...].astype(BF16)
        wu_bf[...] = wu_ref[...].astype(BF16)

    h = h_ref[...]
    a = jnp.dot(h, wg_bf[...], preferred_element_type=F32)
    b = jnp.dot(h, wu_bf[...], preferred_element_type=F32)
    o_ref[...] = (a * jax.nn.sigmoid(a) * b).astype(o_ref.dtype)


def _ffn_up(h, w_gate, w_up, layer, tm=1024, tn=512):
    t, k = h.shape
    n = w_gate.shape[2]
    wspec = pl.BlockSpec((None, k, tn), lambda j, i: (layer, 0, j))
    return pl.pallas_call(
        _ffn_up_kernel,
        grid=(n // tn, t // tm),
        in_specs=[pl.BlockSpec((tm, k), lambda j, i: (i, 0)), wspec, wspec],
        out_specs=pl.BlockSpec((tm, tn), lambda j, i: (i, j)),
        out_shape=jax.ShapeDtypeStruct((t, n), BF16),
        scratch_shapes=[pltpu.VMEM((k, tn), BF16), pltpu.VMEM((k, tn), BF16)],
        compiler_params=_cparams("parallel", "arbitrary"),
        name="ffn_up",
    )(h, w_gate, w_up)


def _mm_kernel(a_ref, b_ref, o_ref):
    o_ref[...] = jnp.dot(a_ref[...], b_ref[...], preferred_element_type=F32).astype(o_ref.dtype)


def _mm(a, b, out_dtype, tm=1024, tn=512):
    t, k = a.shape
    n = b.shape[1]
    return pl.pallas_call(
        _mm_kernel,
        grid=(t // tm, n // tn),
        in_specs=[pl.BlockSpec((tm, k), lambda i, j: (i, 0)), pl.BlockSpec((k, tn), lambda i, j: (0, j))],
        out_specs=pl.BlockSpec((tm, tn), lambda i, j: (i, j)),
        out_shape=jax.ShapeDtypeStruct((t, n), out_dtype),
        compiler_params=_cparams("parallel", "parallel"),
        name="proj_in",
    )(a, b)


def _mm_res_kernel(a_ref, b_ref, r_ref, o_ref, *, scale):
    o_ref[...] = r_ref[...] + scale * jnp.dot(a_ref[...], b_ref[...], preferred_element_type=F32)


def _mm_res(a, b, res, scale, tm=1024, tn=512):
    t, kdim = a.shape
    n = b.shape[1]
    return pl.pallas_call(
        functools.partial(_mm_res_kernel, scale=scale),
        grid=(t // tm, n // tn),
        in_specs=[pl.BlockSpec((tm, kdim), lambda i, j: (i, 0)),
                  pl.BlockSpec((kdim, tn), lambda i, j: (0, j)),
                  pl.BlockSpec((tm, tn), lambda i, j: (i, j))],
        out_specs=pl.BlockSpec((tm, tn), lambda i, j: (i, j)),
        out_shape=jax.ShapeDtypeStruct((t, n), F32),
        compiler_params=_cparams("parallel", "parallel"),
        name="mm_residual",
    )(a, b, res)


def _rope(x, cos2, sin2):
    return x * cos2 + pltpu.roll(x, HEAD_DIM // 2, axis=1) * sin2


def _nsa_prep_kernel(kc_ref, vc_ref, ks_ref, vs_ref, kw_ref, vw_ref, cos_ref, sin_ref, cv_ref, kv_ref):
    cos2 = cos_ref[...]
    sin2 = sin_ref[...]
    for g in range(NSA_KV_HEADS):
        sl = slice(g * HEAD_DIM, (g + 1) * HEAD_DIM)
        cv_ref[g] = _rope(kc_ref[:, sl], cos2, sin2)
        cv_ref[NSA_KV_HEADS + g] = vc_ref[:, sl]
        for j, (ref, roped) in enumerate(((ks_ref, True), (vs_ref, False), (kw_ref, True), (vw_ref, False))):
            x = ref[:, sl]
            if roped:
                x = _rope(x, cos2, sin2)
            c0 = (j * NSA_KV_HEADS + g) * HEAD_DIM
            kv_ref[:, c0:c0 + HEAD_DIM] = x.astype(BF16)


def _nsa_prep(p, cos2, sin2, seq, tile=512):
    t = p.shape[0]
    per_seq = seq // tile
    kvw = KV_WIDTH

    def pspec(cb):
        return pl.BlockSpec((tile, kvw), lambda i, cb=cb: (i, cb * LANE // kvw))

    tab = pl.BlockSpec((tile, HEAD_DIM), lambda i: (i % per_seq, 0))
    return pl.pallas_call(
        _nsa_prep_kernel,
        grid=(t // tile,),
        in_specs=[pspec(CB_NKC), pspec(CB_NVC), pspec(CB_NKS), pspec(CB_NVS), pspec(CB_NKW), pspec(CB_NVW), tab, tab],
        out_specs=[pl.BlockSpec((2 * NSA_KV_HEADS, tile, HEAD_DIM), lambda i: (0, i, 0)),
                   pl.BlockSpec((tile, 4 * kvw), lambda i: (i, 0))],
        out_shape=[jax.ShapeDtypeStruct((2 * NSA_KV_HEADS, t, HEAD_DIM), F32),
                   jax.ShapeDtypeStruct((t, 4 * kvw), BF16)],
        compiler_params=_cparams("parallel"),
        name="nsa_prep",
    )(p, p, p, p, p, p, cos2, sin2)


def _nsa_compress_kernel(t_ref, pe_ref, w1_ref, w2_ref, o_ref):
    x = t_ref[...]
    half = CMP_STRIDE * HEAD_DIM
    nr = x.shape[0]
    u = _dot(x + pe_ref[0:1, :], w1_ref[0:half, :], precision=HIGHEST)
    v = _dot(x + pe_ref[1:2, :], w1_ref[half:2 * half, :], precision=HIGHEST)
    pre = u + pltpu.roll(v, nr - 1, axis=0)
    o_ref[...] = _dot(jax.nn.gelu(pre), w2_ref[...], precision=HIGHEST)


def _nsa_compress(cv, pe, w1, w2, batch, seq):
    nr = seq // CMP_STRIDE
    half = CMP_STRIDE * HEAD_DIM
    t2 = cv.reshape(2 * NSA_KV_HEADS, batch, nr, half)
    return pl.pallas_call(
        _nsa_compress_kernel,
        grid=(2 * NSA_KV_HEADS, batch),
        in_specs=[pl.BlockSpec((None, None, nr, half), lambda c, b: (c, b, 0, 0)),
                  pl.BlockSpec((None, 2, half), lambda c, b: (c // NSA_KV_HEADS, 0, 0)),
                  pl.BlockSpec((None, 2 * half, HEAD_DIM), lambda c, b: (c // NSA_KV_HEADS, 0, 0)),
                  pl.BlockSpec((None, HEAD_DIM, HEAD_DIM), lambda c, b: (c // NSA_KV_HEADS, 0, 0))],
        out_specs=pl.BlockSpec((None, None, nr, HEAD_DIM), lambda c, b: (c, b, 0, 0)),
        out_shape=jax.ShapeDtypeStruct((2 * NSA_KV_HEADS, batch, nr, HEAD_DIM), F32),
        compiler_params=_cparams("parallel", "parallel"),
        name="nsa_compress",
    )(t2, pe.reshape(2, 2, half), w1, w2)


def _masked_softmax(s, mask):
    s = jnp.where(mask, s, MASK_VALUE)
    m = jnp.max(s, axis=-1, keepdims=True)
    e = jnp.where(mask, jnp.exp(s - m), 0.0)
    return e / jnp.maximum(jnp.sum(e, axis=-1, keepdims=True), 1e-30)


def _nsa_attn_kernel(q_ref, gt_ref, cos_ref, sin_ref, kc_ref, vc_ref, ks_ref, vs_ref, kw_ref, vw_ref,
                     o_ref, m_ref, l_ref, acc_ref, *, seq):
    g = pl.program_id(1)
    i = pl.program_id(2)
    qn, r_n, d = Q_BLOCK, NSA_REP, HEAD_DIM
    rows = r_n * qn
    n_blk = seq // SEL_BLOCK
    n_sel = min(N_SEL, n_blk)
    ncr = seq // CMP_STRIDE

    cos2 = cos_ref[...]
    sin2 = sin_ref[...]
    qb = q_ref[...]
    q = jnp.concatenate([_rope(qb[:, r * d:(r + 1) * d], cos2, sin2) * (HEAD_DIM ** -0.5) for r in range(r_n)],
                        axis=0)
    q_bf = q.astype(BF16)
    t0 = i * qn
    t_col = t0 + lax.broadcasted_iota(jnp.int32, (qn, 1), 0)
    t_rows = jnp.concatenate([t_col] * r_n, axis=0)

    kc = kc_ref[...]
    vc = vc_ref[...]
    s_c = _dot3(_split_bf16(q), _split_bf16(kc), ((1,), (1,)))
    cmp_end = lax.broadcasted_iota(jnp.int32, (1, ncr), 1) * CMP_STRIDE + (CMP_BLOCK - 1)
    p_c = _masked_softmax(s_c, cmp_end <= t_rows)
    o_c = _dot(p_c.astype(BF16), vc.astype(BF16))

    pc_sum = p_c[0:qn]
    for r in range(1, r_n):
        pc_sum = pc_sum + p_c[r * qn:(r + 1) * qn]
    m_i = lax.broadcasted_iota(jnp.int32, (n_blk, ncr), 0)
    c_i = lax.broadcasted_iota(jnp.int32, (n_blk, ncr), 1)
    ov = (jnp.minimum(c_i * CMP_STRIDE + CMP_BLOCK, m_i * SEL_BLOCK + SEL_BLOCK)
          - jnp.maximum(c_i * CMP_STRIDE, m_i * SEL_BLOCK))
    ov_t = jnp.where(c_i < ncr - 1, jnp.maximum(ov, 0).astype(F32) / CMP_BLOCK, 0.0)
    ov_bf = ov_t.astype(BF16)
    pc_hi, pc_lo = _split_bf16(pc_sum)
    imp = _dot_nt(ov_bf, pc_hi) + _dot_nt(ov_bf, pc_lo)
    blk = lax.broadcasted_iota(jnp.int32, (n_blk, qn), 0)
    t_row = t0 + lax.broadcasted_iota(jnp.int32, (n_blk, qn), 1)
    cur = t_row // SEL_BLOCK
    forced = (blk == 0) | (blk == cur) | (blk == cur - 1)
    future = blk * SEL_BLOCK > t_row
    imp = jnp.where(forced, FORCE_VALUE, jnp.where(future, -FORCE_VALUE, imp))
    rank = jnp.zeros((n_blk, qn), F32)
    for j in range(n_blk):
        row = imp[j:j + 1, :]
        beats = (row > imp) | ((row == imp) & (blk > j))
        rank = rank + jnp.where(beats, 1.0, 0.0)
    sel_t = jnp.where(rank < n_sel, 1.0, 0.0)
    sel = sel_t.T.astype(BF16)

    m_ref[...] = jnp.full((rows, 1), MASK_VALUE, F32)
    l_ref[...] = jnp.zeros((rows, 1), F32)
    acc_ref[...] = jnp.zeros((rows, d), F32)
    n_tiles = (t0 + qn + SEL_TILE - 1) // SEL_TILE

    def sel_step(kt, carry):
        k0 = pl.multiple_of(kt * SEL_TILE, SEL_TILE)
        s = _dot_nt(q_bf, ks_ref[pl.ds(k0, SEL_TILE), :])
        kpos = k0 + lax.broadcasted_iota(jnp.int32, (1, SEL_TILE), 1)
        e_blk = lax.broadcasted_iota(jnp.int32, (n_blk, SEL_TILE), 0)
        e_key = k0 + lax.broadcasted_iota(jnp.int32, (n_blk, SEL_TILE), 1)
        expand = jnp.where(e_blk == e_key // SEL_BLOCK, 1.0, 0.0).astype(BF16)
        chosen = _dot(sel, expand)
        ok_q = (chosen > 0.5) & (kpos <= t_col)
        ok = jnp.concatenate([ok_q] * r_n, axis=0)
        s = jnp.where(ok, s, MASK_VALUE)
        m_old = m_ref[...]
        m_new = jnp.maximum(m_old, jnp.max(s, axis=-1, keepdims=True))
        alpha = jnp.exp(m_old - m_new)
        p = jnp.exp(s - m_new)
        l_ref[...] = alpha * l_ref[...] + jnp.sum(p, axis=-1, keepdims=True)
        acc_ref[...] = alpha * acc_ref[...] + _dot(p.astype(BF16), vs_ref[pl.ds(k0, SEL_TILE), :])
        m_ref[...] = m_new
        return carry

    lax.fori_loop(0, n_tiles, sel_step, 0)
    o_s = acc_ref[...] / jnp.maximum(l_ref[...], 1e-30)

    wlen = WINDOW + qn
    w0 = pl.multiple_of(jnp.maximum(t0 - WINDOW, 0), qn)
    kw = kw_ref[pl.ds(w0, wlen), :]
    vw = vw_ref[pl.ds(w0, wlen), :]
    s_w = _dot_nt(q_bf, kw)
    wpos = w0 + lax.broadcasted_iota(jnp.int32, (1, wlen), 1)
    p_w = _masked_softmax(s_w, (wpos <= t_rows) & (wpos > t_rows - WINDOW))
    o_w = _dot(p_w.astype(BF16), vw)

    gates = jax.nn.sigmoid(gt_ref[...])
    outs = []
    for r in range(r_n):
        sl = slice(r * qn, (r + 1) * qn)
        o_r = 0.0
        for c, o_b in enumerate((o_c, o_s, o_w)):
            lane = r * 3 + c
            gcol = jnp.where(g == 0, gates[:, lane:lane + 1], gates[:, 3 * r_n + lane:3 * r_n + lane + 1])
            o_r = o_r + gcol * o_b[sl]
        outs.append(o_r)
    o_ref[...] = jnp.concatenate(outs, axis=1).astype(o_ref.dtype)


def _nsa_attn(p, cmp_kv, kvbf, cos2, sin2, batch, seq):
    t = p.shape[0]
    n_qb = seq // Q_BLOCK
    ncr = seq // CMP_STRIDE
    gw = NSA_REP * HEAD_DIM
    rows = NSA_REP * Q_BLOCK

    def kv_spec(j):
        return pl.BlockSpec((seq, HEAD_DIM), lambda b, g, i, j=j: (b, j * NSA_KV_HEADS + g))

    return pl.pallas_call(
        functools.partial(_nsa_attn_kernel, seq=seq),
        grid=(batch, NSA_KV_HEADS, n_qb),
        in_specs=[pl.BlockSpec((Q_BLOCK, gw), lambda b, g, i: (b * n_qb + i, CB_NQ // NSA_REP + g)),
                  pl.BlockSpec((Q_BLOCK, LANE), lambda b, g, i: (b * n_qb + i, CB_NGT)),
                  pl.BlockSpec((Q_BLOCK, HEAD_DIM), lambda b, g, i: (i, 0)),
                  pl.BlockSpec((Q_BLOCK, HEAD_DIM), lambda b, g, i: (i, 0)),
                  pl.BlockSpec((None, None, ncr, HEAD_DIM), lambda b, g, i: (g, b, 0, 0)),
                  pl.BlockSpec((None, None, ncr, HEAD_DIM), lambda b, g, i: (NSA_KV_HEADS + g, b, 0, 0)),
                  kv_spec(0), kv_spec(1), kv_spec(2), kv_spec(3)],
        out_specs=pl.BlockSpec((Q_BLOCK, gw), lambda b, g, i: (b * n_qb + i, g)),
        out_shape=jax.ShapeDtypeStruct((t, NSA_WIDTH), BF16),
        scratch_shapes=[pltpu.VMEM((rows, 1), F32), pltpu.VMEM((rows, 1), F32), pltpu.VMEM((rows, HEAD_DIM), F32)],
        compiler_params=_cparams("parallel", "parallel", "arbitrary"),
        name="nsa_attn",
    )(p, p, cos2, sin2, cmp_kv, cmp_kv, kvbf, kvbf, kvbf, kvbf)


def _tri(n, lower_inclusive=True):
    r = lax.broadcasted_iota(jnp.int32, (n, n), 0)
    c = lax.broadcasted_iota(jnp.int32, (n, n), 1)
    return r, c


def _softplus(x):
    return jnp.maximum(x, 0.0) + jnp.log1p(jnp.exp(-jnp.abs(x)))


def _gated_rmsnorm(o, norm_w, gate):
    y = o * lax.rsqrt(jnp.mean(o * o, axis=-1, keepdims=True) + EPS)
    return (y * norm_w) * (gate * jax.nn.sigmoid(gate))


GDN_TILE = 256
GDN_SCAN_TILE = 128


def _split_bf16(x):
    hi = x.astype(BF16)
    return hi, (x - hi.astype(F32)).astype(BF16)


def _dot3(a, b, dims=((1,), (0,))):
    return _dot(a[0], b[0], dims) + (_dot(a[0], b[1], dims) + _dot(a[1], b[0], dims))


def _chunk_cumsum(x, chunk):
    pos = lax.broadcasted_iota(jnp.int32, (x.shape[0], 1), 0) % chunk
    s = 1
    while s < chunk:
        x = x + jnp.where(pos >= s, pltpu.roll(x, s, axis=0), 0.0)
        s *= 2
    return x


def _gdn_prep_kernel(q_ref, k_ref, v_ref, ab_ref, wconv_ref, alog_ref, dtb_ref,
                     u_ref, w_ref, qd_ref, att_ref, kdt_ref, dl_ref, tail_ref):
    c_n, d, h_n, tb = GDN_CHUNK, HEAD_DIM, GDN_HEADS, GDN_TILE
    nch = tb // c_n
    w = h_n * d

    @pl.when(pl.program_id(1) == 0)
    def _():
        tail_ref[...] = jnp.zeros_like(tail_ref)

    x = jnp.concatenate([q_ref[...], k_ref[...], v_ref[...]], axis=1)
    xcat = jnp.concatenate([tail_ref[...], x], axis=0)
    wc = wconv_ref[...]
    conv = None
    for j in range(CONV_WIDTH):
        shift = CONV_WIDTH - 1 - j
        xs = xcat if shift == 0 else pltpu.roll(xcat, shift, axis=0)
        term = xs[8:8 + tb] * wc[j:j + 1, :]
        conv = term if conv is None else conv + term
    tail_ref[...] = x[tb - 8:tb]
    act = conv * jax.nn.sigmoid(conv)

    ab = ab_ref[...]
    g_all = -jnp.exp(alog_ref[...]) * _softplus(ab + dtb_ref[...])
    beta_all = jax.nn.sigmoid(ab)
    gc_all = _chunk_cumsum(g_all, c_n)
    gl_all = jnp.concatenate([jnp.broadcast_to(gc_all[(c + 1) * c_n - 1:(c + 1) * c_n], (c_n, LANE))
                              for c in range(nch)], axis=0)
    gct_all = gc_all.T
    dl_ref[...] = jnp.exp(gl_all)
    r_i, c_i = _tri(tb)
    same = (r_i // c_n) == (c_i // c_n)
    causal = same & (r_i >= c_i)
    strict = same & (r_i > c_i)
    eye = jnp.where(r_i == c_i, 1.0, 0.0)
    lane_i = lax.broadcasted_iota(jnp.int32, (d, LANE), 1)

    for h in range(h_n):
        sl = slice(h * d, (h + 1) * d)
        xq = act[:, h * d:(h + 1) * d]
        xk = act[:, w + h * d:w + (h + 1) * d]
        v = act[:, 2 * w + h * d:2 * w + (h + 1) * d]
        q = xq * lax.rsqrt(jnp.sum(xq * xq, axis=-1, keepdims=True) + EPS) * (HEAD_DIM ** -0.5)
        k = xk * lax.rsqrt(jnp.sum(xk * xk, axis=-1, keepdims=True) + EPS)
        beta = beta_all[:, h_n + h:h_n + h + 1]
        gc = gc_all[:, h:h + 1]
        gc_row = gct_all[h:h + 1, :]
        decay = jnp.where(causal, jnp.exp(jnp.where(causal, gc - gc_row, 0.0)), 0.0)
        kb = k * beta
        qk = _dot_nt(jnp.concatenate([q, kb], axis=0).astype(BF16), k.astype(BF16))
        attn = qk[:tb] * decay
        a_mat = jnp.where(strict, qk[tb:] * decay, 0.0)
        egc = jnp.exp(gc)
        rhs = jnp.concatenate([v * beta, kb * egc], axis=1)
        pw = -a_mat
        t_inv = eye + pw
        pw_s = _split_bf16(pw)
        pw = _dot3(pw_s, pw_s)
        pw_b = pw.astype(BF16)
        t_inv = t_inv + _dot3(_split_bf16(t_inv), _split_bf16(pw))
        span = 4
        while span < c_n:
            pw = _dot(pw_b, pw_b)
            pw_b = pw.astype(BF16)
            t_inv = t_inv + _dot(t_inv.astype(BF16), pw_b)
            span *= 2
        sol = _dot3(_split_bf16(t_inv), _split_bf16(rhs))
        u_ref[:, sl] = sol[:, :d]
        w_ref[:, sl] = sol[:, d:].astype(BF16)
        qd_ref[:, sl] = (q * egc).astype(BF16)
        att_ref[:, sl] = jnp.concatenate([attn[j * LANE:(j + 1) * LANE, j * LANE:(j + 1) * LANE]
                                          for j in range(tb // LANE)], axis=0).astype(BF16)
        kdt = (k * jnp.exp(gl_all[:, h:h + 1] - gc)).T
        for c in range(nch):
            blk = kdt[:, (c // 2) * LANE:(c // 2 + 1) * LANE]
            if c % 2:
                blk = pltpu.roll(blk, c_n, axis=1)
            kdt_ref[c, h] = jnp.where(lane_i < c_n, blk, 0.0).astype(BF16)


def _gdn_scan_kernel(u_ref, w_ref, qd_ref, att_ref, kdt_ref, dl_ref, z_ref, nw_ref, o_ref, state_ref):
    c_n, d, h_n = GDN_CHUNK, HEAD_DIM, GDN_HEADS
    n_b = u_ref.shape[0]

    @pl.when(pl.program_id(0) == 0)
    def _():
        state_ref[...] = jnp.zeros_like(state_ref)

    nw = nw_ref[...]
    for c in range(GDN_SCAN_TILE // c_n):
        rs = slice(c * c_n, (c + 1) * c_n)
        for b in range(n_b):
            for h in range(h_n):
                sl = slice(h * d, (h + 1) * d)
                state = state_ref[b * h_n + h]
                r = _dot(jnp.concatenate([w_ref[b, rs, sl], qd_ref[b, rs, sl]], axis=0), state.astype(BF16))
                v_new = (u_ref[b, rs, sl] - r[:c_n]).astype(BF16)
                vv = jnp.concatenate([v_new, v_new], axis=0)
                o = r[c_n:] + _dot(att_ref[b, rs, sl], vv)
                state_ref[b * h_n + h] = (state * dl_ref[b, c * c_n:c * c_n + 1, h:h + 1]
                                          + _dot(kdt_ref[b, c, h], vv))
                o_ref[b, rs, sl] = _gated_rmsnorm(o, nw, z_ref[b, rs, sl]).astype(o_ref.dtype)


def _gdn(p, w_conv, a_log, dt_bias, norm_w, batch, seq):
    t = p.shape[0]
    c_n, h_n, d = GDN_CHUNK, GDN_HEADS, HEAD_DIM
    w = h_n * d
    tb, ts = GDN_TILE, GDN_SCAN_TILE
    pad = jnp.zeros((LANE - h_n,), F32)
    alog = jnp.concatenate([a_log, pad]).reshape(1, LANE)
    dtb = jnp.concatenate([dt_bias, pad]).reshape(1, LANE)

    def whole(shape):
        return pl.BlockSpec(shape, lambda b, c: (0,) * len(shape))

    n_t = seq // tb

    def pspec(cb, width):
        return pl.BlockSpec((tb, width), lambda b, i, cb=cb, width=width: (b * n_t + i, cb * LANE // width))

    row_w = pl.BlockSpec((tb, w), lambda b, i: (b * n_t + i, 0))
    u, wm, qd, att, kdt, dl = pl.pallas_call(
        _gdn_prep_kernel,
        grid=(batch, n_t),
        in_specs=[pspec(CB_GQ, w), pspec(CB_GK, w), pspec(CB_GV, w), pspec(CB_GAB, LANE),
                  whole((CONV_WIDTH, 3 * w)), whole((1, LANE)), whole((1, LANE))],
        out_specs=[row_w, row_w, row_w, row_w,
                   pl.BlockSpec((tb // c_n, h_n, d, LANE), lambda b, i: (b * n_t + i, 0, 0, 0)),
                   pl.BlockSpec((tb, LANE), lambda b, i: (b * n_t + i, 0))],
        out_shape=[jax.ShapeDtypeStruct((t, w), F32), jax.ShapeDtypeStruct((t, w), BF16),
                   jax.ShapeDtypeStruct((t, w), BF16), jax.ShapeDtypeStruct((t, w), BF16),
                   jax.ShapeDtypeStruct((t // c_n, h_n, d, LANE), BF16), jax.ShapeDtypeStruct((t, LANE), F32)],
        scratch_shapes=[pltpu.VMEM((8, 3 * w), F32)],
        compiler_params=_cparams("parallel", "arbitrary"),
        name="gdn_prep",
    )(p, p, p, p, w_conv, alog, dtb)

    def per_batch(a):
        return a.reshape((batch, a.shape[0] // batch) + a.shape[1:])

    row_s = pl.BlockSpec((batch, ts, w), lambda i: (0, i, 0))
    y = pl.pallas_call(
        _gdn_scan_kernel,
        grid=(seq // ts,),
        in_specs=[row_s, row_s, row_s, row_s,
                  pl.BlockSpec((batch, ts // c_n, h_n, d, LANE), lambda i: (0, i, 0, 0, 0)),
                  pl.BlockSpec((batch, ts, LANE), lambda i: (0, i, 0)),
                  pl.BlockSpec((batch, ts, w), lambda i: (0, i, CB_GZ * LANE // w)),
                  pl.BlockSpec((1, d), lambda i: (0, 0))],
        out_specs=row_s,
        out_shape=jax.ShapeDtypeStruct((batch, seq, w), BF16),
        scratch_shapes=[pltpu.VMEM((batch * h_n, d, d), F32)],
        compiler_params=_cparams("arbitrary"),
        name="gdn_scan",
    )(per_batch(u), per_batch(wm), per_batch(qd), per_batch(att), per_batch(kdt), per_batch(dl), per_batch(p),
      norm_w.reshape(1, d))
    return y.reshape(t, w)


HGRN_SUB = 16


def _hgrn_kernel(q_ref, f_ref, i_ref, g_ref, lb_ref, nw_ref, o_ref, state_ref):
    c_n, d, h_n, sb = HGRN_CHUNK, HEAD_DIM, HGRN_HEADS, HGRN_SUB
    ci = pl.program_id(1)

    @pl.when(ci == 0)
    def _():
        state_ref[...] = jnp.zeros_like(state_ref)

    lane_i = lax.broadcasted_iota(jnp.int32, (sb, c_n), 1)
    nw = nw_ref[...]
    outs = []
    for h in range(h_n):
        sl = slice(h * d, (h + 1) * d)
        lb = lb_ref[:, sl]
        f_gate = lb + (1.0 - lb) * jax.nn.sigmoid(f_ref[:, sl])
        log_f = jnp.log(jnp.maximum(f_gate, MIN_FORGET))
        k = 1.0 - f_gate
        q = q_ref[:, sl]
        v = i_ref[:, sl]
        bcum = _chunk_cumsum(log_f, c_n)
        a_rows = []
        for sbi in range(c_n // sb):
            r0 = sbi * sb
            q_s = q[r0:r0 + sb]
            b_s = bcum[r0:r0 + sb]
            a_blk = jnp.zeros((sb, c_n), F32)
            if sbi > 0:
                b_ref0 = bcum[r0:r0 + 1]
                q_t = q_s * jnp.exp(b_s - b_ref0)
                k_t = k * jnp.exp(jnp.minimum(b_ref0 - bcum, 0.0))
                a_blk = jnp.where(lane_i < r0, _dot3(_split_bf16(q_t), _split_bf16(k_t), ((1,), (1,))), 0.0)
            for jj in range(sb):
                j = r0 + jj
                ok = lax.broadcasted_iota(jnp.int32, (sb, 1), 0) >= jj
                e = jnp.where(ok, jnp.exp(jnp.where(ok, b_s - bcum[j:j + 1], 0.0)), 0.0)
                col = jnp.sum(q_s * k[j:j + 1] * e, axis=-1, keepdims=True)
                a_blk = jnp.where(lane_i == j, col, a_blk)
            a_rows.append(a_blk)
        a_mat = jnp.concatenate(a_rows, axis=0)
        state_t = state_ref[h]
        v_bf = v.astype(BF16)
        o = _dot_nt((q * jnp.exp(bcum)).astype(BF16), state_t.astype(BF16)) + _dot(a_mat.astype(BF16), v_bf)
        b_last = bcum[c_n - 1:c_n]
        k_dec = k * jnp.exp(b_last - bcum)
        state_ref[h] = state_t * jnp.exp(b_last) + _dot(v.T.astype(BF16), k_dec.astype(BF16))
        outs.append(_gated_rmsnorm(o, nw, g_ref[:, sl]))
    o_ref[...] = jnp.concatenate(outs, axis=1).astype(o_ref.dtype)


def _hgrn(p, lb, norm_w, batch, seq):
    t = p.shape[0]
    c_n, h_n, d = HGRN_CHUNK, HGRN_HEADS, HEAD_DIM
    w = h_n * d
    n_c = seq // c_n

    def pspec(cb):
        return pl.BlockSpec((c_n, w), lambda b, c, cb=cb: (b * n_c + c, cb * LANE // w))

    return pl.pallas_call(
        _hgrn_kernel,
        grid=(batch, n_c),
        in_specs=[pspec(CB_HQ), pspec(CB_HF), pspec(CB_HI), pspec(CB_HG),
                  pl.BlockSpec((1, w), lambda b, c: (0, 0)), pl.BlockSpec((1, d), lambda b, c: (0, 0))],
        out_specs=pl.BlockSpec((c_n, w), lambda b, c: (b * n_c + c, 0)),
        out_shape=jax.ShapeDtypeStruct((t, w), BF16),
        scratch_shapes=[pltpu.VMEM((h_n, d, d), F32)],
        compiler_params=_cparams("parallel", "arbitrary"),
        name="hgrn",
    )(p, p, p, p, lb.reshape(1, w), norm_w.reshape(1, d))


def _round_up(x, m):
    return -(-x // m) * m


def _pack_plan():
    offs = [int(v) for v in np.concatenate([[0], np.cumsum(np.array(IN_SIZES))])]
    plan = []
    dst = 0
    for j in (8, 9, 10, 13, 14, 15, 16, 17, 1, 2, 3, 4, 5, 6):
        plan.append((offs[j], IN_SIZES[j], dst))
        dst += IN_SIZES[j]
    plan.append((offs[7], IN_SIZES[7], CB_NGT * LANE))
    plan.append((offs[11], IN_SIZES[11] + IN_SIZES[12], CB_GAB * LANE))
    plan.append((offs[0], IN_SIZES[0], CB_NQ * LANE))
    return tuple(plan)


def _pack_kernel(w_ref, o_ref):
    for src, width, dst in _pack_plan():
        a0 = src // LANE * LANE
        off = src - a0
        span = _round_up(off + width, LANE)
        x = w_ref[:, a0:a0 + span]
        if off:
            x = pltpu.roll(x, span - off, axis=1)
        wout = _round_up(width, LANE)
        x = x[:, :wout]
        if width % LANE:
            x = jnp.where(lax.broadcasted_iota(jnp.int32, x.shape, 1) < width, x, 0.0)
        o_ref[:, dst:dst + wout] = x.astype(o_ref.dtype)


def _pack_w_in(w_in, layer, tr=256):
    _, k, n = w_in.shape
    return pl.pallas_call(
        _pack_kernel,
        grid=(k // tr,),
        in_specs=[pl.BlockSpec((None, tr, _round_up(n, LANE)), lambda i: (layer, i, 0))],
        out_specs=pl.BlockSpec((tr, P_WIDTH), lambda i: (i, 0)),
        out_shape=jax.ShapeDtypeStruct((k, P_WIDTH), BF16),
        compiler_params=_cparams("parallel"),
        name="pack_w_in",
    )(w_in)


def _rope_tables(seq):
    inv = 1.0 / (ROPE_THETA ** (jnp.arange(0, HEAD_DIM, 2, dtype=F32) / HEAD_DIM))
    ang = jnp.arange(seq, dtype=F32)[:, None] * inv[None, :]
    cos, sin = jnp.cos(ang), jnp.sin(ang)
    return jnp.concatenate([cos, cos], axis=1), jnp.concatenate([-sin, sin], axis=1)


def _mixers(p, batch, seq, layer_params):
    (pe_k, pe_v, ck1, ck2, cv1, cv2, conv_w, a_log, dt_bias, gdn_norm, lb, hgrn_norm, cos2, sin2) = layer_params
    cv, kvbf = _nsa_prep(p, cos2, sin2, seq)
    cmp_kv = _nsa_compress(cv, jnp.stack([pe_k, pe_v]), jnp.stack([ck1, cv1]), jnp.stack([ck2, cv2]), batch, seq)
    y_nsa = _nsa_attn(p, cmp_kv, kvbf, cos2, sin2, batch, seq)
    y_gdn = _gdn(p, conv_w, a_log, dt_bias, gdn_norm, batch, seq)
    y_hgrn = _hgrn(p, lb, hgrn_norm, batch, seq)
    return jnp.concatenate([y_nsa, y_gdn, y_hgrn], axis=1)


def kernel(x, ffn1_norm, ffn1_gate, ffn1_up, ffn1_down, mix_norm, w_in, w_out, nsa_pe_k, nsa_pe_v, nsa_ck1, nsa_ck2, nsa_cv1, nsa_cv2, gdn_conv, gdn_a_log, gdn_dt_bias, gdn_norm, hgrn_lb, hgrn_norm, ffn2_norm, ffn2_gate, ffn2_up, ffn2_down, final_norm):
    batch, seq, dm = x.shape
    depth = w_in.shape[0]
    cos2, sin2 = _rope_tables(seq)
    p_lb = jax.nn.softmax(hgrn_lb.astype(F32), axis=0)
    lb_all = jnp.cumsum(p_lb, axis=0) - p_lb[0:1]
    xt = x.reshape(batch * seq, dm)

    def ffn(xt, norm, wg, wu, wd, l):
        h = _rmsnorm(xt, norm[l], BF16)
        act = _ffn_up(h, wg, wu, l)
        return _mm_res(act, wd[l].astype(BF16), xt, 0.5)

    for l in range(depth):
        xt = ffn(xt, ffn1_norm, ffn1_gate, ffn1_up, ffn1_down, l)
        h = _rmsnorm(xt, mix_norm[l], BF16)
        p = _mm(h, _pack_w_in(w_in, l), F32)
        y = _mixers(p, batch, seq, (nsa_pe_k[l], nsa_pe_v[l], nsa_ck1[l], nsa_ck2[l], nsa_cv1[l], nsa_cv2[l],
                                    gdn_conv[l], gdn_a_log[l], gdn_dt_bias[l], gdn_norm[l], lb_all[l], hgrn_norm[l],
                                    cos2, sin2))
        xt = _mm_res(y, w_out[l].astype(BF16), xt, 1.0)
        xt = ffn(xt, ffn2_norm, ffn2_gate, ffn2_up, ffn2_down, l)
    return _rmsnorm(xt, final_norm, F32).reshape(batch, seq, dm)
```

```python
import functools

import jax
import jax.numpy as jnp
import numpy as np
from jax import lax
from jax.experimental import pallas as pl
from jax.experimental.pallas import tpu as pltpu

F32 = jnp.float32
BF16 = jnp.bfloat16
HIGHEST = lax.Precision.HIGHEST

D_MODEL = 2048
HEAD_DIM = 128
ROPE_THETA = 10000.0
EPS = 1e-6
MASK_VALUE = -1e30
FORCE_VALUE = 1e9
MIN_FORGET = 1e-6
NSA_HEADS = 6
NSA_KV_HEADS = 2
NSA_REP = NSA_HEADS // NSA_KV_HEADS
CMP_BLOCK = 32
CMP_STRIDE = 16
SEL_BLOCK = 64
N_SEL = 16
WINDOW = 512
Q_BLOCK = 256
GDN_HEADS = 5
CONV_WIDTH = 4
GDN_CHUNK = 64
HGRN_HEADS = 5
HGRN_CHUNK = 64
D_FF = 5632

NSA_WIDTH = NSA_HEADS * HEAD_DIM
KV_WIDTH = NSA_KV_HEADS * HEAD_DIM
GDN_WIDTH = GDN_HEADS * HEAD_DIM
HGRN_WIDTH = HGRN_HEADS * HEAD_DIM
MIX_WIDTH = NSA_WIDTH + GDN_WIDTH + HGRN_WIDTH
IN_SIZES = (NSA_WIDTH, KV_WIDTH, KV_WIDTH, KV_WIDTH, KV_WIDTH, KV_WIDTH, KV_WIDTH, 3 * NSA_HEADS,
            GDN_WIDTH, GDN_WIDTH, GDN_WIDTH, GDN_HEADS, GDN_HEADS, GDN_WIDTH,
            HGRN_WIDTH, HGRN_WIDTH, HGRN_WIDTH, HGRN_WIDTH)

SEL_TILE = 1024
LANE = 128
VMEM_LIMIT = 48 * 1024 * 1024

CB_GQ = 0
CB_GK = 5
CB_GV = 10
CB_GZ = 15
CB_HQ = 20
CB_HF = 25
CB_HI = 30
CB_HG = 35
CB_NKC = 40
CB_NVC = 42
CB_NKS = 44
CB_NVS = 46
CB_NKW = 48
CB_NVW = 50
CB_NGT = 52
CB_GAB = 53
CB_NQ = 54
P_BLOCKS = 60
P_WIDTH = P_BLOCKS * LANE


def _cparams(*sem):
    return pltpu.CompilerParams(dimension_semantics=sem, vmem_limit_bytes=VMEM_LIMIT)


def _dot(a, b, dims=((1,), (0,)), precision=None):
    return lax.dot_general(a, b, (dims, ((), ())), precision=precision, preferred_element_type=F32)


def _dot_nt(a, b, precision=None):
    return _dot(a, b, ((1,), (1,)), precision)


def _dot_tn(a, b, precision=None):
    return _dot(a, b, ((0,), (0,)), precision)


def _rmsnorm_kernel(x_ref, g_ref, o_ref):
    x = x_ref[...]
    y = x * lax.rsqrt(jnp.mean(x * x, axis=-1, keepdims=True) + EPS)
    o_ref[...] = (y * g_ref[...]).astype(o_ref.dtype)


def _rmsnorm(x, g, out_dtype, tm=512):
    t, d = x.shape
    return pl.pallas_call(
        _rmsnorm_kernel,
        grid=(t // tm,),
        in_specs=[pl.BlockSpec((tm, d), lambda i: (i, 0)), pl.BlockSpec((1, d), lambda i: (0, 0))],
        out_specs=pl.BlockSpec((tm, d), lambda i: (i, 0)),
        out_shape=jax.ShapeDtypeStruct((t, d), out_dtype),
        compiler_params=_cparams("parallel"),
        name="rmsnorm",
    )(x, g.reshape(1, d))


def _ffn_up_kernel(h_ref, wg_ref, wu_ref, o_ref, wg_bf, wu_bf):
    @pl.when(pl.program_id(1) == 0)
    def _():
        wg_bf[...] = wg_ref[...].astype(BF16)
        wu_bf[...] = wu_ref[...].astype(BF16)

    h = h_ref[...]
    a = jnp.dot(h, wg_bf[...], preferred_element_type=F32)
    b = jnp.dot(h, wu_bf[...], preferred_element_type=F32)
    o_ref[...] = (a * jax.nn.sigmoid(a) * b).astype(o_ref.dtype)


def _ffn_up(h, w_gate, w_up, layer, tm=1024, tn=512):
    t, k = h.shape
    n = w_gate.shape[2]
    wspec = pl.BlockSpec((None, k, tn), lambda j, i: (layer, 0, j))
    return pl.pallas_call(
        _ffn_up_kernel,
        grid=(n // tn, t // tm),
        in_specs=[pl.BlockSpec((tm, k), lambda j, i: (i, 0)), wspec, wspec],
        out_specs=pl.BlockSpec((tm, tn), lambda j, i: (i, j)),
        out_shape=jax.ShapeDtypeStruct((t, n), BF16),
        scratch_shapes=[pltpu.VMEM((k, tn), BF16), pltpu.VMEM((k, tn), BF16)],
        compiler_params=_cparams("parallel", "arbitrary"),
        name="ffn_up",
    )(h, w_gate, w_up)


def _mm_kernel(a_ref, b_ref, o_ref):
    o_ref[...] = jnp.dot(a_ref[...], b_ref[...], preferred_element_type=F32).astype(o_ref.dtype)


def _mm(a, b, out_dtype, tm=1024, tn=1536):
    t, k = a.shape
    n = b.shape[1]
    return pl.pallas_call(
        _mm_kernel,
        grid=(t // tm, n // tn),
        in_specs=[pl.BlockSpec((tm, k), lambda i, j: (i, 0)), pl.BlockSpec((k, tn), lambda i, j: (0, j))],
        out_specs=pl.BlockSpec((tm, tn), lambda i, j: (i, j)),
        out_shape=jax.ShapeDtypeStruct((t, n), out_dtype),
        compiler_params=_cparams("parallel", "parallel"),
        name="proj_in",
    )(a, b)


def _mm_res_kernel(a_ref, b_ref, r_ref, o_ref, *, scale):
    o_ref[...] = r_ref[...] + scale * jnp.dot(a_ref[...], b_ref[...], preferred_element_type=F32)


def _mm_res(a, b, res, scale, tm=1024, tn=512):
    t, kdim = a.shape
    n = b.shape[1]
    return pl.pallas_call(
        functools.partial(_mm_res_kernel, scale=scale),
        grid=(t // tm, n // tn),
        in_specs=[pl.BlockSpec((tm, kdim), lambda i, j: (i, 0)),
                  pl.BlockSpec((kdim, tn), lambda i, j: (0, j)),
                  pl.BlockSpec((tm, tn), lambda i, j: (i, j))],
        out_specs=pl.BlockSpec((tm, tn), lambda i, j: (i, j)),
        out_shape=jax.ShapeDtypeStruct((t, n), F32),
        compiler_params=_cparams("parallel", "parallel"),
        name="mm_residual",
    )(a, b, res)


def _rope(x, cos2, sin2):
    return x * cos2 + pltpu.roll(x, HEAD_DIM // 2, axis=1) * sin2


def _nsa_prep_kernel(kc_ref, vc_ref, ks_ref, vs_ref, kw_ref, vw_ref, cos_ref, sin_ref, cv_ref, kv_ref):
    cos2 = cos_ref[...]
    sin2 = sin_ref[...]
    for g in range(NSA_KV_HEADS):
        sl = slice(g * HEAD_DIM, (g + 1) * HEAD_DIM)
        cv_ref[g] = _rope(kc_ref[:, sl], cos2, sin2)
        cv_ref[NSA_KV_HEADS + g] = vc_ref[:, sl]
        for j, (ref, roped) in enumerate(((ks_ref, True), (vs_ref, False), (kw_ref, True), (vw_ref, False))):
            x = ref[:, sl]
            if roped:
                x = _rope(x, cos2, sin2)
            c0 = (j * NSA_KV_HEADS + g) * HEAD_DIM
            kv_ref[:, c0:c0 + HEAD_DIM] = x.astype(BF16)


def _nsa_prep(p, cos2, sin2, seq, tile=512):
    t = p.shape[0]
    per_seq = seq // tile
    kvw = KV_WIDTH

    def pspec(cb):
        return pl.BlockSpec((tile, kvw), lambda i, cb=cb: (i, cb * LANE // kvw))

    tab = pl.BlockSpec((tile, HEAD_DIM), lambda i: (i % per_seq, 0))
    return pl.pallas_call(
        _nsa_prep_kernel,
        grid=(t // tile,),
        in_specs=[pspec(CB_NKC), pspec(CB_NVC), pspec(CB_NKS), pspec(CB_NVS), pspec(CB_NKW), pspec(CB_NVW), tab, tab],
        out_specs=[pl.BlockSpec((2 * NSA_KV_HEADS, tile, HEAD_DIM), lambda i: (0, i, 0)),
                   pl.BlockSpec((tile, 4 * kvw), lambda i: (i, 0))],
        out_shape=[jax.ShapeDtypeStruct((2 * NSA_KV_HEADS, t, HEAD_DIM), F32),
                   jax.ShapeDtypeStruct((t, 4 * kvw), BF16)],
        compiler_params=_cparams("parallel"),
        name="nsa_prep",
    )(p, p, p, p, p, p, cos2, sin2)


def _nsa_compress_kernel(t_ref, pe_ref, w1_ref, w2_ref, o_ref):
    x = t_ref[...]
    half = CMP_STRIDE * HEAD_DIM
    nr = x.shape[0]
    u = _dot(x + pe_ref[0:1, :], w1_ref[0:half, :], precision=HIGHEST)
    v = _dot(x + pe_ref[1:2, :], w1_ref[half:2 * half, :], precision=HIGHEST)
    pre = u + pltpu.roll(v, nr - 1, axis=0)
    o_ref[...] = _dot(jax.nn.gelu(pre), w2_ref[...], precision=HIGHEST)


def _nsa_compress(cv, pe, w1, w2, batch, seq):
    nr = seq // CMP_STRIDE
    half = CMP_STRIDE * HEAD_DIM
    t2 = cv.reshape(2 * NSA_KV_HEADS, batch, nr, half)
    return pl.pallas_call(
        _nsa_compress_kernel,
        grid=(2 * NSA_KV_HEADS, batch),
        in_specs=[pl.BlockSpec((None, None, nr, half), lambda c, b: (c, b, 0, 0)),
                  pl.BlockSpec((None, 2, half), lambda c, b: (c // NSA_KV_HEADS, 0, 0)),
                  pl.BlockSpec((None, 2 * half, HEAD_DIM), lambda c, b: (c // NSA_KV_HEADS, 0, 0)),
                  pl.BlockSpec((None, HEAD_DIM, HEAD_DIM), lambda c, b: (c // NSA_KV_HEADS, 0, 0))],
        out_specs=pl.BlockSpec((None, None, nr, HEAD_DIM), lambda c, b: (c, b, 0, 0)),
        out_shape=jax.ShapeDtypeStruct((2 * NSA_KV_HEADS, batch, nr, HEAD_DIM), F32),
        compiler_params=_cparams("parallel", "parallel"),
        name="nsa_compress",
    )(t2, pe.reshape(2, 2, half), w1, w2)


def _masked_softmax(s, mask):
    s = jnp.where(mask, s, MASK_VALUE)
    m = jnp.max(s, axis=-1, keepdims=True)
    e = jnp.where(mask, jnp.exp(s - m), 0.0)
    return e / jnp.maximum(jnp.sum(e, axis=-1, keepdims=True), 1e-30)


def _nsa_attn_kernel(q_ref, gt_ref, cos_ref, sin_ref, kc_ref, vc_ref, ks_ref, vs_ref, kw_ref, vw_ref,
                     o_ref, m_ref, l_ref, acc_ref, *, seq):
    g = pl.program_id(1)
    i = pl.program_id(2)
    qn, r_n, d = Q_BLOCK, NSA_REP, HEAD_DIM
    rows = r_n * qn
    n_blk = seq // SEL_BLOCK
    n_sel = min(N_SEL, n_blk)
    ncr = seq // CMP_STRIDE

    cos2 = cos_ref[...]
    sin2 = sin_ref[...]
    qb = q_ref[...]
    q = jnp.concatenate([_rope(qb[:, r * d:(r + 1) * d], cos2, sin2) * (HEAD_DIM ** -0.5) for r in range(r_n)],
                        axis=0)
    q_bf = q.astype(BF16)
    t0 = i * qn
    t_col = t0 + lax.broadcasted_iota(jnp.int32, (qn, 1), 0)
    t_rows = jnp.concatenate([t_col] * r_n, axis=0)

    kc = kc_ref[...]
    vc = vc_ref[...]
    s_c = _dot3(_split_bf16(q), _split_bf16(kc), ((1,), (1,)))
    cmp_end = lax.broadcasted_iota(jnp.int32, (1, ncr), 1) * CMP_STRIDE + (CMP_BLOCK - 1)
    p_c = _masked_softmax(s_c, cmp_end <= t_rows)
    o_c = _dot(p_c.astype(BF16), vc.astype(BF16))

    pc_sum = p_c[0:qn]
    for r in range(1, r_n):
        pc_sum = pc_sum + p_c[r * qn:(r + 1) * qn]
    m_i = lax.broadcasted_iota(jnp.int32, (n_blk, ncr), 0)
    c_i = lax.broadcasted_iota(jnp.int32, (n_blk, ncr), 1)
    ov = (jnp.minimum(c_i * CMP_STRIDE + CMP_BLOCK, m_i * SEL_BLOCK + SEL_BLOCK)
          - jnp.maximum(c_i * CMP_STRIDE, m_i * SEL_BLOCK))
    ov_t = jnp.where(c_i < ncr - 1, jnp.maximum(ov, 0).astype(F32) / CMP_BLOCK, 0.0)
    ov_bf = ov_t.astype(BF16)
    pc_hi, pc_lo = _split_bf16(pc_sum)
    imp = _dot_nt(ov_bf, pc_hi) + _dot_nt(ov_bf, pc_lo)
    blk = lax.broadcasted_iota(jnp.int32, (n_blk, qn), 0)
    t_row = t0 + lax.broadcasted_iota(jnp.int32, (n_blk, qn), 1)
    cur = t_row // SEL_BLOCK
    forced = (blk == 0) | (blk == cur) | (blk == cur - 1)
    future = blk * SEL_BLOCK > t_row
    imp = jnp.where(forced, FORCE_VALUE, jnp.where(future, -FORCE_VALUE, imp))
    rank = jnp.zeros((n_blk, qn), F32)
    for j in range(n_blk):
        row = imp[j:j + 1, :]
        beats = (row > imp) | ((row == imp) & (blk > j))
        rank = rank + jnp.where(beats, 1.0, 0.0)
    sel_t = jnp.where(rank < n_sel, 1.0, 0.0)
    sel = sel_t.T.astype(BF16)

    m_ref[...] = jnp.full((rows, 1), MASK_VALUE, F32)
    l_ref[...] = jnp.zeros((rows, 1), F32)
    acc_ref[...] = jnp.zeros((rows, d), F32)
    n_tiles = (t0 + qn + SEL_TILE - 1) // SEL_TILE

    def sel_step(kt, carry):
        k0 = pl.multiple_of(kt * SEL_TILE, SEL_TILE)
        s = _dot_nt(q_bf, ks_ref[pl.ds(k0, SEL_TILE), :])
        kpos = k0 + lax.broadcasted_iota(jnp.int32, (1, SEL_TILE), 1)
        e_blk = lax.broadcasted_iota(jnp.int32, (n_blk, SEL_TILE), 0)
        e_key = k0 + lax.broadcasted_iota(jnp.int32, (n_blk, SEL_TILE), 1)
        expand = jnp.where(e_blk == e_key // SEL_BLOCK, 1.0, 0.0).astype(BF16)
        chosen = _dot(sel, expand)
        ok_q = (chosen > 0.5) & (kpos <= t_col)
        ok = jnp.concatenate([ok_q] * r_n, axis=0)
        s = jnp.where(ok, s, MASK_VALUE)
        m_old = m_ref[...]
        m_new = jnp.maximum(m_old, jnp.max(s, axis=-1, keepdims=True))
        alpha = jnp.exp(m_old - m_new)
        p = jnp.exp(s - m_new)
        l_ref[...] = alpha * l_ref[...] + jnp.sum(p, axis=-1, keepdims=True)
        acc_ref[...] = alpha * acc_ref[...] + _dot(p.astype(BF16), vs_ref[pl.ds(k0, SEL_TILE), :])
        m_ref[...] = m_new
        return carry

    lax.fori_loop(0, n_tiles, sel_step, 0)
    o_s = acc_ref[...] / jnp.maximum(l_ref[...], 1e-30)

    wlen = WINDOW + qn
    w0 = pl.multiple_of(jnp.maximum(t0 - WINDOW, 0), qn)
    kw = kw_ref[pl.ds(w0, wlen), :]
    vw = vw_ref[pl.ds(w0, wlen), :]
    s_w = _dot_nt(q_bf, kw)
    wpos = w0 + lax.broadcasted_iota(jnp.int32, (1, wlen), 1)
    p_w = _masked_softmax(s_w, (wpos <= t_rows) & (wpos > t_rows - WINDOW))
    o_w = _dot(p_w.astype(BF16), vw)

    gates = jax.nn.sigmoid(gt_ref[...])
    outs = []
    for r in range(r_n):
        sl = slice(r * qn, (r + 1) * qn)
        o_r = 0.0
        for c, o_b in enumerate((o_c, o_s, o_w)):
            lane = r * 3 + c
            gcol = jnp.where(g == 0, gates[:, lane:lane + 1], gates[:, 3 * r_n + lane:3 * r_n + lane + 1])
            o_r = o_r + gcol * o_b[sl]
        outs.append(o_r)
    o_ref[...] = jnp.concatenate(outs, axis=1).astype(o_ref.dtype)


def _nsa_attn(p, cmp_kv, kvbf, cos2, sin2, batch, seq):
    t = p.shape[0]
    n_qb = seq // Q_BLOCK
    ncr = seq // CMP_STRIDE
    gw = NSA_REP * HEAD_DIM
    rows = NSA_REP * Q_BLOCK

    def kv_spec(j):
        return pl.BlockSpec((seq, HEAD_DIM), lambda b, g, i, j=j: (b, j * NSA_KV_HEADS + g))

    return pl.pallas_call(
        functools.partial(_nsa_attn_kernel, seq=seq),
        grid=(batch, NSA_KV_HEADS, n_qb),
        in_specs=[pl.BlockSpec((Q_BLOCK, gw), lambda b, g, i: (b * n_qb + i, CB_NQ // NSA_REP + g)),
                  pl.BlockSpec((Q_BLOCK, LANE), lambda b, g, i: (b * n_qb + i, CB_NGT)),
                  pl.BlockSpec((Q_BLOCK, HEAD_DIM), lambda b, g, i: (i, 0)),
                  pl.BlockSpec((Q_BLOCK, HEAD_DIM), lambda b, g, i: (i, 0)),
                  pl.BlockSpec((None, None, ncr, HEAD_DIM), lambda b, g, i: (g, b, 0, 0)),
                  pl.BlockSpec((None, None, ncr, HEAD_DIM), lambda b, g, i: (NSA_KV_HEADS + g, b, 0, 0)),
                  kv_spec(0), kv_spec(1), kv_spec(2), kv_spec(3)],
        out_specs=pl.BlockSpec((Q_BLOCK, gw), lambda b, g, i: (b * n_qb + i, g)),
        out_shape=jax.ShapeDtypeStruct((t, NSA_WIDTH), BF16),
        scratch_shapes=[pltpu.VMEM((rows, 1), F32), pltpu.VMEM((rows, 1), F32), pltpu.VMEM((rows, HEAD_DIM), F32)],
        compiler_params=_cparams("parallel", "parallel", "arbitrary"),
        name="nsa_attn",
    )(p, p, cos2, sin2, cmp_kv, cmp_kv, kvbf, kvbf, kvbf, kvbf)


def _tri(n, lower_inclusive=True):
    r = lax.broadcasted_iota(jnp.int32, (n, n), 0)
    c = lax.broadcasted_iota(jnp.int32, (n, n), 1)
    return r, c


def _softplus(x):
    return jnp.maximum(x, 0.0) + jnp.log1p(jnp.exp(-jnp.abs(x)))


def _gated_rmsnorm(o, norm_w, gate):
    y = o * lax.rsqrt(jnp.mean(o * o, axis=-1, keepdims=True) + EPS)
    return (y * norm_w) * (gate * jax.nn.sigmoid(gate))


GDN_TILE = 256
GDN_SCAN_TILE = 128


def _split_bf16(x):
    hi = x.astype(BF16)
    return hi, (x - hi.astype(F32)).astype(BF16)


def _dot3(a, b, dims=((1,), (0,))):
    return _dot(a[0], b[0], dims) + (_dot(a[0], b[1], dims) + _dot(a[1], b[0], dims))


def _chunk_cumsum(x, chunk):
    pos = lax.broadcasted_iota(jnp.int32, (x.shape[0], 1), 0) % chunk
    s = 1
    while s < chunk:
        x = x + jnp.where(pos >= s, pltpu.roll(x, s, axis=0), 0.0)
        s *= 2
    return x


def _gdn_prep_kernel(q_ref, k_ref, v_ref, ab_ref, wconv_ref, alog_ref, dtb_ref,
                     u_ref, w_ref, qd_ref, att_ref, kdt_ref, dl_ref, tail_ref):
    c_n, d, h_n, tb = GDN_CHUNK, HEAD_DIM, GDN_HEADS, GDN_TILE
    nch = tb // c_n
    w = h_n * d

    @pl.when(pl.program_id(1) == 0)
    def _():
        tail_ref[...] = jnp.zeros_like(tail_ref)

    x = jnp.concatenate([q_ref[...], k_ref[...], v_ref[...]], axis=1)
    xcat = jnp.concatenate([tail_ref[...], x], axis=0)
    wc = wconv_ref[...]
    conv = None
    for j in range(CONV_WIDTH):
        shift = CONV_WIDTH - 1 - j
        xs = xcat if shift == 0 else pltpu.roll(xcat, shift, axis=0)
        term = xs[8:8 + tb] * wc[j:j + 1, :]
        conv = term if conv is None else conv + term
    tail_ref[...] = x[tb - 8:tb]
    act = conv * jax.nn.sigmoid(conv)

    ab = ab_ref[...]
    g_all = -jnp.exp(alog_ref[...]) * _softplus(ab + dtb_ref[...])
    beta_all = jax.nn.sigmoid(ab)
    gc_all = _chunk_cumsum(g_all, c_n)
    gl_all = jnp.concatenate([jnp.broadcast_to(gc_all[(c + 1) * c_n - 1:(c + 1) * c_n], (c_n, LANE))
                              for c in range(nch)], axis=0)
    gct_all = gc_all.T
    dl_ref[...] = jnp.exp(gl_all)
    r_i, c_i = _tri(tb)
    same = (r_i // c_n) == (c_i // c_n)
    causal = same & (r_i >= c_i)
    strict = same & (r_i > c_i)
    eye = jnp.where(r_i == c_i, 1.0, 0.0)
    lane_i = lax.broadcasted_iota(jnp.int32, (d, LANE), 1)

    for h in range(h_n):
        sl = slice(h * d, (h + 1) * d)
        xq = act[:, h * d:(h + 1) * d]
        xk = act[:, w + h * d:w + (h + 1) * d]
        v = act[:, 2 * w + h * d:2 * w + (h + 1) * d]
        q = xq * lax.rsqrt(jnp.sum(xq * xq, axis=-1, keepdims=True) + EPS) * (HEAD_DIM ** -0.5)
        k = xk * lax.rsqrt(jnp.sum(xk * xk, axis=-1, keepdims=True) + EPS)
        beta = beta_all[:, h_n + h:h_n + h + 1]
        gc = gc_all[:, h:h + 1]
        gc_row = gct_all[h:h + 1, :]
        decay = jnp.where(causal, jnp.exp(jnp.where(causal, gc - gc_row, 0.0)), 0.0)
        kb = k * beta
        qk = _dot_nt(jnp.concatenate([q, kb], axis=0).astype(BF16), k.astype(BF16))
        attn = qk[:tb] * decay
        a_mat = jnp.where(strict, qk[tb:] * decay, 0.0)
        egc = jnp.exp(gc)
        rhs = jnp.concatenate([v * beta, kb * egc], axis=1)
        pw = -a_mat
        t_inv = eye + pw
        pw_s = _split_bf16(pw)
        pw = _dot3(pw_s, pw_s)
        pw_b = pw.astype(BF16)
        t_inv = t_inv + _dot3(_split_bf16(t_inv), _split_bf16(pw))
        span = 4
        while span < c_n:
            pw = _dot(pw_b, pw_b)
            pw_b = pw.astype(BF16)
            t_inv = t_inv + _dot(t_inv.astype(BF16), pw_b)
            span *= 2
        sol = _dot3(_split_bf16(t_inv), _split_bf16(rhs))
        u_ref[:, sl] = sol[:, :d]
        w_ref[:, sl] = sol[:, d:].astype(BF16)
        qd_ref[:, sl] = (q * egc).astype(BF16)
        att_ref[:, sl] = jnp.concatenate([attn[j * LANE:(j + 1) * LANE, j * LANE:(j + 1) * LANE]
                                          for j in range(tb // LANE)], axis=0).astype(BF16)
        kdt = (k * jnp.exp(gl_all[:, h:h + 1] - gc)).T
        for c in range(nch):
            blk = kdt[:, (c // 2) * LANE:(c // 2 + 1) * LANE]
            if c % 2:
                blk = pltpu.roll(blk, c_n, axis=1)
            kdt_ref[c, h] = jnp.where(lane_i < c_n, blk, 0.0).astype(BF16)


def _gdn_scan_kernel(u_ref, w_ref, qd_ref, att_ref, kdt_ref, dl_ref, z_ref, nw_ref, o_ref, state_ref):
    c_n, d, h_n = GDN_CHUNK, HEAD_DIM, GDN_HEADS
    n_b = u_ref.shape[0]

    @pl.when(pl.program_id(0) == 0)
    def _():
        state_ref[...] = jnp.zeros_like(state_ref)

    nw = nw_ref[...]
    for c in range(GDN_SCAN_TILE // c_n):
        rs = slice(c * c_n, (c + 1) * c_n)
        for b in range(n_b):
            for h in range(h_n):
                sl = slice(h * d, (h + 1) * d)
                state = state_ref[b * h_n + h]
                r = _dot(jnp.concatenate([w_ref[b, rs, sl], qd_ref[b, rs, sl]], axis=0), state.astype(BF16))
                v_new = (u_ref[b, rs, sl] - r[:c_n]).astype(BF16)
                vv = jnp.concatenate([v_new, v_new], axis=0)
                o = r[c_n:] + _dot(att_ref[b, rs, sl], vv)
                state_ref[b * h_n + h] = (state * dl_ref[b, c * c_n:c * c_n + 1, h:h + 1]
                                          + _dot(kdt_ref[b, c, h], vv))
                o_ref[b, rs, sl] = _gated_rmsnorm(o, nw, z_ref[b, rs, sl]).astype(o_ref.dtype)


def _gdn(p, w_conv, a_log, dt_bias, norm_w, batch, seq):
    t = p.shape[0]
    c_n, h_n, d = GDN_CHUNK, GDN_HEADS, HEAD_DIM
    w = h_n * d
    tb, ts = GDN_TILE, GDN_SCAN_TILE
    pad = jnp.zeros((LANE - h_n,), F32)
    alog = jnp.concatenate([a_log, pad]).reshape(1, LANE)
    dtb = jnp.concatenate([dt_bias, pad]).reshape(1, LANE)

    def whole(shape):
        return pl.BlockSpec(shape, lambda b, c: (0,) * len(shape))

    n_t = seq // tb

    def pspec(cb, width):
        return pl.BlockSpec((tb, width), lambda b, i, cb=cb, width=width: (b * n_t + i, cb * LANE // width))

    row_w = pl.BlockSpec((tb, w), lambda b, i: (b * n_t + i, 0))
    u, wm, qd, att, kdt, dl = pl.pallas_call(
        _gdn_prep_kernel,
        grid=(batch, n_t),
        in_specs=[pspec(CB_GQ, w), pspec(CB_GK, w), pspec(CB_GV, w), pspec(CB_GAB, LANE),
                  whole((CONV_WIDTH, 3 * w)), whole((1, LANE)), whole((1, LANE))],
        out_specs=[row_w, row_w, row_w, row_w,
                   pl.BlockSpec((tb // c_n, h_n, d, LANE), lambda b, i: (b * n_t + i, 0, 0, 0)),
                   pl.BlockSpec((tb, LANE), lambda b, i: (b * n_t + i, 0))],
        out_shape=[jax.ShapeDtypeStruct((t, w), F32), jax.ShapeDtypeStruct((t, w), BF16),
                   jax.ShapeDtypeStruct((t, w), BF16), jax.ShapeDtypeStruct((t, w), BF16),
                   jax.ShapeDtypeStruct((t // c_n, h_n, d, LANE), BF16), jax.ShapeDtypeStruct((t, LANE), F32)],
        scratch_shapes=[pltpu.VMEM((8, 3 * w), F32)],
        compiler_params=_cparams("parallel", "arbitrary"),
        name="gdn_prep",
    )(p, p, p, p, w_conv, alog, dtb)

    def per_batch(a):
        return a.reshape((batch, a.shape[0] // batch) + a.shape[1:])

    row_s = pl.BlockSpec((batch, ts, w), lambda i: (0, i, 0))
    y = pl.pallas_call(
        _gdn_scan_kernel,
        grid=(seq // ts,),
        in_specs=[row_s, row_s, row_s, row_s,
                  pl.BlockSpec((batch, ts // c_n, h_n, d, LANE), lambda i: (0, i, 0, 0, 0)),
                  pl.BlockSpec((batch, ts, LANE), lambda i: (0, i, 0)),
                  pl.BlockSpec((batch, ts, w), lambda i: (0, i, CB_GZ * LANE // w)),
                  pl.BlockSpec((1, d), lambda i: (0, 0))],
        out_specs=row_s,
        out_shape=jax.ShapeDtypeStruct((batch, seq, w), BF16),
        scratch_shapes=[pltpu.VMEM((batch * h_n, d, d), F32)],
        compiler_params=_cparams("arbitrary"),
        name="gdn_scan",
    )(per_batch(u), per_batch(wm), per_batch(qd), per_batch(att), per_batch(kdt), per_batch(dl), per_batch(p),
      norm_w.reshape(1, d))
    return y.reshape(t, w)


HGRN_SUB = 16


def _hgrn_kernel(q_ref, f_ref, i_ref, g_ref, lb_ref, nw_ref, o_ref, state_ref):
    c_n, d, h_n, sb = HGRN_CHUNK, HEAD_DIM, HGRN_HEADS, HGRN_SUB
    ci = pl.program_id(1)

    @pl.when(ci == 0)
    def _():
        state_ref[...] = jnp.zeros_like(state_ref)

    lane_i = lax.broadcasted_iota(jnp.int32, (sb, c_n), 1)
    nw = nw_ref[...]
    outs = []
    for h in range(h_n):
        sl = slice(h * d, (h + 1) * d)
        lb = lb_ref[:, sl]
        f_gate = lb + (1.0 - lb) * jax.nn.sigmoid(f_ref[:, sl])
        log_f = jnp.log(jnp.maximum(f_gate, MIN_FORGET))
        k = 1.0 - f_gate
        q = q_ref[:, sl]
        v = i_ref[:, sl]
        bcum = _chunk_cumsum(log_f, c_n)
        a_rows = []
        for sbi in range(c_n // sb):
            r0 = sbi * sb
            q_s = q[r0:r0 + sb]
            b_s = bcum[r0:r0 + sb]
            a_blk = jnp.zeros((sb, c_n), F32)
            if sbi > 0:
                b_ref0 = bcum[r0:r0 + 1]
                q_t = q_s * jnp.exp(b_s - b_ref0)
                k_t = k * jnp.exp(jnp.minimum(b_ref0 - bcum, 0.0))
                a_blk = jnp.where(lane_i < r0, _dot3(_split_bf16(q_t), _split_bf16(k_t), ((1,), (1,))), 0.0)
            for jj in range(sb):
                j = r0 + jj
                ok = lax.broadcasted_iota(jnp.int32, (sb, 1), 0) >= jj
                e = jnp.where(ok, jnp.exp(jnp.where(ok, b_s - bcum[j:j + 1], 0.0)), 0.0)
                col = jnp.sum(q_s * k[j:j + 1] * e, axis=-1, keepdims=True)
                a_blk = jnp.where(lane_i == j, col, a_blk)
            a_rows.append(a_blk)
        a_mat = jnp.concatenate(a_rows, axis=0)
        state_t = state_ref[h]
        v_bf = v.astype(BF16)
        o = _dot_nt((q * jnp.exp(bcum)).astype(BF16), state_t.astype(BF16)) + _dot(a_mat.astype(BF16), v_bf)
        b_last = bcum[c_n - 1:c_n]
        k_dec = k * jnp.exp(b_last - bcum)
        state_ref[h] = state_t * jnp.exp(b_last) + _dot(v.T.astype(BF16), k_dec.astype(BF16))
        outs.append(_gated_rmsnorm(o, nw, g_ref[:, sl]))
    o_ref[...] = jnp.concatenate(outs, axis=1).astype(o_ref.dtype)


def _hgrn(p, lb, norm_w, batch, seq):
    t = p.shape[0]
    c_n, h_n, d = HGRN_CHUNK, HGRN_HEADS, HEAD_DIM
    w = h_n * d
    n_c = seq // c_n

    def pspec(cb):
        return pl.BlockSpec((c_n, w), lambda b, c, cb=cb: (b * n_c + c, cb * LANE // w))

    return pl.pallas_call(
        _hgrn_kernel,
        grid=(batch, n_c),
        in_specs=[pspec(CB_HQ), pspec(CB_HF), pspec(CB_HI), pspec(CB_HG),
                  pl.BlockSpec((1, w), lambda b, c: (0, 0)), pl.BlockSpec((1, d), lambda b, c: (0, 0))],
        out_specs=pl.BlockSpec((c_n, w), lambda b, c: (b * n_c + c, 0)),
        out_shape=jax.ShapeDtypeStruct((t, w), BF16),
        scratch_shapes=[pltpu.VMEM((h_n, d, d), F32)],
        compiler_params=_cparams("parallel", "arbitrary"),
        name="hgrn",
    )(p, p, p, p, lb.reshape(1, w), norm_w.reshape(1, d))


def _round_up(x, m):
    return -(-x // m) * m


def _pack_plan():
    offs = [int(v) for v in np.concatenate([[0], np.cumsum(np.array(IN_SIZES))])]
    plan = []
    dst = 0
    for j in (8, 9, 10, 13, 14, 15, 16, 17, 1, 2, 3, 4, 5, 6):
        plan.append((offs[j], IN_SIZES[j], dst))
        dst += IN_SIZES[j]
    plan.append((offs[7], IN_SIZES[7], CB_NGT * LANE))
    plan.append((offs[11], IN_SIZES[11] + IN_SIZES[12], CB_GAB * LANE))
    plan.append((offs[0], IN_SIZES[0], CB_NQ * LANE))
    return tuple(plan)


def _pack_kernel(w_ref, o_ref):
    for src, width, dst in _pack_plan():
        a0 = src // LANE * LANE
        off = src - a0
        span = _round_up(off + width, LANE)
        x = w_ref[:, a0:a0 + span]
        if off:
            x = pltpu.roll(x, span - off, axis=1)
        wout = _round_up(width, LANE)
        x = x[:, :wout]
        if width % LANE:
            x = jnp.where(lax.broadcasted_iota(jnp.int32, x.shape, 1) < width, x, 0.0)
        o_ref[:, dst:dst + wout] = x.astype(o_ref.dtype)


def _pack_w_in(w_in, layer, tr=256):
    _, k, n = w_in.shape
    return pl.pallas_call(
        _pack_kernel,
        grid=(k // tr,),
        in_specs=[pl.BlockSpec((None, tr, _round_up(n, LANE)), lambda i: (layer, i, 0))],
        out_specs=pl.BlockSpec((tr, P_WIDTH), lambda i: (i, 0)),
        out_shape=jax.ShapeDtypeStruct((k, P_WIDTH), BF16),
        compiler_params=_cparams("parallel"),
        name="pack_w_in",
    )(w_in)


def _rope_tables(seq):
    inv = 1.0 / (ROPE_THETA ** (jnp.arange(0, HEAD_DIM, 2, dtype=F32) / HEAD_DIM))
    ang = jnp.arange(seq, dtype=F32)[:, None] * inv[None, :]
    cos, sin = jnp.cos(ang), jnp.sin(ang)
    return jnp.concatenate([cos, cos], axis=1), jnp.concatenate([-sin, sin], axis=1)


def _mixers(p, batch, seq, layer_params):
    (pe_k, pe_v, ck1, ck2, cv1, cv2, conv_w, a_log, dt_bias, gdn_norm, lb, hgrn_norm, cos2, sin2) = layer_params
    cv, kvbf = _nsa_prep(p, cos2, sin2, seq)
    cmp_kv = _nsa_compress(cv, jnp.stack([pe_k, pe_v]), jnp.stack([ck1, cv1]), jnp.stack([ck2, cv2]), batch, seq)
    y_nsa = _nsa_attn(p, cmp_kv, kvbf, cos2, sin2, batch, seq)
    y_gdn = _gdn(p, conv_w, a_log, dt_bias, gdn_norm, batch, seq)
    y_hgrn = _hgrn(p, lb, hgrn_norm, batch, seq)
    return jnp.concatenate([y_nsa, y_gdn, y_hgrn], axis=1)


def kernel(x, ffn1_norm, ffn1_gate, ffn1_up, ffn1_down, mix_norm, w_in, w_out, nsa_pe_k, nsa_pe_v, nsa_ck1, nsa_ck2, nsa_cv1, nsa_cv2, gdn_conv, gdn_a_log, gdn_dt_bias, gdn_norm, hgrn_lb, hgrn_norm, ffn2_norm, ffn2_gate, ffn2_up, ffn2_down, final_norm):
    batch, seq, dm = x.shape
    depth = w_in.shape[0]
    cos2, sin2 = _rope_tables(seq)
    p_lb = jax.nn.softmax(hgrn_lb.astype(F32), axis=0)
    lb_all = jnp.cumsum(p_lb, axis=0) - p_lb[0:1]
    xt = x.reshape(batch * seq, dm)

    def ffn(xt, norm, wg, wu, wd, l):
        h = _rmsnorm(xt, norm[l], BF16)
        act = _ffn_up(h, wg, wu, l)
        return _mm_res(act, wd[l].astype(BF16), xt, 0.5)

    for l in range(depth):
        xt = ffn(xt, ffn1_norm, ffn1_gate, ffn1_up, ffn1_down, l)
        h = _rmsnorm(xt, mix_norm[l], BF16)
        p = _mm(h, _pack_w_in(w_in, l), F32)
        y = _mixers(p, batch, seq, (nsa_pe_k[l], nsa_pe_v[l], nsa_ck1[l], nsa_ck2[l], nsa_cv1[l], nsa_cv2[l],
                                    gdn_conv[l], gdn_a_log[l], gdn_dt_bias[l], gdn_norm[l], lb_all[l], hgrn_norm[l],
                                    cos2, sin2))
        xt = _mm_res(y, w_out[l].astype(BF16), xt, 1.0, tn=1024)
        xt = ffn(xt, ffn2_norm, ffn2_gate, ffn2_up, ffn2_down, l)
    return _rmsnorm(xt, final_norm, F32).reshape(batch, seq, dm)
```
